```python
import jax, jax.numpy as jnp
from jax import lax
import numpy as np

D_MODEL = 1024
BATCH = 8
SEQ = 4096
DEPTH = 2

N_BRANCH = 4
BRANCH_W = D_MODEL // 2
GLA_HEADS = 4
GLA_DV = BRANCH_W // GLA_HEADS
GLA_DK = GLA_DV // 2
GLA_KW = GLA_HEADS * GLA_DK
GLA_GATE_RANK = 16
GLA_LOGIT_NORM = 16.0
GLA_CHUNK = 64
RWKV_HEAD = 64
RWKV_HEADS = BRANCH_W // RWKV_HEAD
RWKV_W_LORA = 64
RWKV_A_LORA = 64
RWKV_G_LORA = 128
RWKV_GN_EPS = 64e-5
FOX_HEADS = 8
FOX_DH = BRANCH_W // FOX_HEADS
FOX_QBLOCK = 128
MASK_VALUE = -1e30
HGRN_HEADS = 4
HGRN_EXPAND = 128
HGRN_DV = BRANCH_W // HGRN_HEADS
HGRN_KW = HGRN_HEADS * HGRN_EXPAND
HGRN_CHUNK = 64
N_EXPERTS = 16
N_GROUPS = 4
EXPERTS_PER_GROUP = N_EXPERTS // N_GROUPS
TOP_K = 2
GROUP_SCORE_TOPK = 2
D_FF_EXPERT = D_MODEL // 2
DN_ALPHA = (2.0 * DEPTH) ** 0.25
DN_BETA = (8.0 * DEPTH) ** -0.25
N_ADA = 6

GLA_COLS = 2 * GLA_KW + 2 * BRANCH_W + GLA_GATE_RANK
RWKV_COLS = 3 * BRANCH_W + RWKV_W_LORA + RWKV_A_LORA + RWKV_G_LORA
FOX_COLS = 3 * BRANCH_W + FOX_HEADS
HGRN_COLS = 2 * HGRN_KW + 2 * BRANCH_W
N_IN_COLS = GLA_COLS + RWKV_COLS + FOX_COLS + HGRN_COLS

kernel_name = "hybrid_gla_rwkv7_fox_hgrn2_moe_block"


def _split(t, sizes):
    return jnp.split(t, [int(s) for s in np.cumsum(sizes)[:-1]], axis=-1)


def _heads(t, n):
    b, s, w = t.shape
    return t.reshape(b, s, n, w // n).transpose(0, 2, 1, 3)


def _merge(t):
    b, n, s, d = t.shape
    return t.transpose(0, 2, 1, 3).reshape(b, s, n * d)


def _layer_norm(t, g=None, b=None, eps=1e-5):
    tf = t.astype(jnp.float32)
    mu = tf.mean(-1, keepdims=True)
    var = jnp.square(tf - mu).mean(-1, keepdims=True)
    y = (tf - mu) * lax.rsqrt(var + eps)
    if g is not None:
        y = y * g + b
    return y.astype(t.dtype)


def _rms_norm(t, g, eps=1e-6):
    tf = t.astype(jnp.float32)
    y = tf * lax.rsqrt(jnp.mean(jnp.square(tf), -1, keepdims=True) + eps) * g
    return y.astype(t.dtype)


def chunk_gla(q, k, v, log_g, chunk):
    out_dtype = v.dtype
    bsz, nh, t_len, dk = q.shape
    dv = v.shape[-1]
    n_chunks = t_len // chunk

    def to_chunks(a):
        return jnp.moveaxis(a.astype(jnp.float32).reshape(bsz, nh, n_chunks, chunk, a.shape[-1]), 2, 0)

    causal = jnp.tril(jnp.ones((chunk, chunk), bool))[:, :, None]

    def step(state, inp):
        qi, ki, vi, gi = inp
        b = jnp.cumsum(gi, axis=-2)
        o_inter = jnp.einsum('bhck,bhkv->bhcv', qi * jnp.exp(b), state)
        diff = b[..., :, None, :] - b[..., None, :, :]
        decay = jnp.where(causal, jnp.exp(jnp.where(causal, diff, 0.0)), 0.0)
        att = jnp.einsum('bhik,bhjk,bhijk->bhij', qi, ki, decay)
        o = o_inter + jnp.einsum('bhij,bhjv->bhiv', att, vi)
        b_last = b[..., -1:, :]
        state = jnp.exp(b_last)[..., 0, :, None] * state + jnp.einsum(
            'bhck,bhcv->bhkv', ki * jnp.exp(b_last - b), vi)
        return state, o

    s0 = jnp.zeros((bsz, nh, dk, dv), jnp.float32)
    _, o = lax.scan(step, s0, tuple(map(to_chunks, (q, k, v, log_g))))
    return jnp.moveaxis(o, 0, 2).reshape(bsz, nh, t_len, dv).astype(out_dtype)


def rwkv7_scan(r, log_w, k, v, kk, a):
    out_dtype = v.dtype
    bsz, nh, _, n = r.shape

    def to_time(t):
        return jnp.moveaxis(t.astype(jnp.float32), 2, 0)

    def step(state, inp):
        r_t, lw_t, k_t, v_t, kk_t, a_t = inp
        sa = jnp.einsum('bhvk,bhk->bhv', state, -kk_t)
        state = (state * jnp.exp(lw_t)[:, :, None, :]
                 + sa[..., :, None] * (kk_t * a_t)[..., None, :]
                 + v_t[..., :, None] * k_t[..., None, :])
        return state, jnp.einsum('bhvk,bhk->bhv', state, r_t)

    s0 = jnp.zeros((bsz, nh, n, n), jnp.float32)
    _, y = lax.scan(step, s0, tuple(map(to_time, (r, log_w, k, v, kk, a))))
    return jnp.moveaxis(y, 0, 2).astype(out_dtype)


def fox_attention(q, k, v, log_f):
    bsz, nh, t_len, dh = q.shape
    n_blocks = t_len // FOX_QBLOCK
    cum_f = jnp.cumsum(log_f.astype(jnp.float32), axis=-1)
    q_blocks = jnp.moveaxis(q.reshape(bsz, nh, n_blocks, FOX_QBLOCK, dh), 2, 0)
    f_blocks = jnp.moveaxis(cum_f.reshape(bsz, nh, n_blocks, FOX_QBLOCK), 2, 0)
    key_pos = jnp.arange(t_len)
    scale = dh ** -0.5

    def one_block(args):
        blk, qi, fi = args
        s = jnp.einsum('bhqd,bhkd->bhqk', qi, k).astype(jnp.float32) * scale
        s = s + fi[..., :, None] - cum_f[..., None, :]
        q_pos = blk * FOX_QBLOCK + jnp.arange(FOX_QBLOCK)
        s = jnp.where(key_pos[None, :] <= q_pos[:, None], s, MASK_VALUE)
        p = jax.nn.softmax(s, axis=-1)
        return jnp.einsum('bhqk,bhkd->bhqd', p.astype(v.dtype), v)

    o = lax.map(one_block, (jnp.arange(n_blocks), q_blocks, f_blocks))
    return jnp.moveaxis(o, 0, 2).reshape(bsz, nh, t_len, dh)


def gla_branch(cols, alpha_up, alpha_b, norm_g):
    q, k, v, g, al = _split(cols, [GLA_KW, GLA_KW, BRANCH_W, BRANCH_W, GLA_GATE_RANK])
    log_a = jax.nn.log_sigmoid((al @ alpha_up + alpha_b).astype(jnp.float32)) / GLA_LOGIT_NORM
    o = chunk_gla(_heads(q, GLA_HEADS) * GLA_DK ** -0.5, _heads(k, GLA_HEADS),
                  _heads(v, GLA_HEADS), _heads(log_a, GLA_HEADS), GLA_CHUNK)
    return _merge(_rms_norm(o, norm_g)) * jax.nn.silu(g)


def rwkv7_branch(cols, mu, w0, w2, a0, a2, g2, k_k, k_a, r_k, ln_g, ln_b):
    prev = jnp.pad(cols, ((0, 0), (1, 0), (0, 0)))[:, :-1]
    xs = cols + (prev - cols) * mu
    r, k, v, wl, al, gl = _split(xs, [BRANCH_W, BRANCH_W, BRANCH_W, RWKV_W_LORA, RWKV_A_LORA, RWKV_G_LORA])
    w_raw = -jax.nn.softplus(-(w0 + jnp.tanh(wl) @ w2)) - 0.5
    log_w = -jnp.exp(w_raw.astype(jnp.float32))
    a = jax.nn.sigmoid(a0 + al @ a2)
    g = jax.nn.sigmoid(gl) @ g2
    kk = _heads(k * k_k, RWKV_HEADS)
    kk = kk / jnp.maximum(jnp.sqrt(jnp.sum(jnp.square(kk), -1, keepdims=True)), 1e-12)
    k = k * (1 + (a - 1) * k_a)
    rh, kh, vh = _heads(r, RWKV_HEADS), _heads(k, RWKV_HEADS), _heads(v, RWKV_HEADS)
    y = rwkv7_scan(rh, _heads(log_w, RWKV_HEADS), kh, vh, kk, _heads(a, RWKV_HEADS))
    y = _merge(_layer_norm(y, eps=RWKV_GN_EPS)) * ln_g + ln_b
    bonus = jnp.sum(rh * kh * r_k[:, None, :], -1, keepdims=True) * vh
    return (y + _merge(bonus)) * g


def fox_branch(cols, f_bias):
    q, k, v, fl = _split(cols, [BRANCH_W, BRANCH_W, BRANCH_W, FOX_HEADS])
    log_f = jax.nn.log_sigmoid((fl + f_bias).astype(jnp.float32)).transpose(0, 2, 1)
    o = fox_attention(_heads(q, FOX_HEADS), _heads(k, FOX_HEADS), _heads(v, FOX_HEADS), log_f)
    return _merge(o)


def hgrn2_branch(cols, lb, norm_g):
    q, fg, i_in, g = _split(cols, [HGRN_KW, HGRN_KW, BRANCH_W, BRANCH_W])
    fg = fg.astype(jnp.float32)
    lbf = lb.astype(jnp.float32)
    f = lbf + (1 - lbf) * jax.nn.sigmoid(fg)
    log_f = jnp.log(f)
    one_minus_f = (1 - lbf) * jax.nn.sigmoid(-fg)
    o = chunk_gla(_heads(q, HGRN_HEADS), _heads(one_minus_f, HGRN_HEADS),
                  _heads(i_in, HGRN_HEADS), _heads(log_f, HGRN_HEADS), HGRN_CHUNK)
    return _merge(_rms_norm(o, norm_g)) * jax.nn.silu(g)


def grouped_moe(h, router_w, router_b, w_g, w_u, w_d):
    bsz, t_len, _ = h.shape
    probs = jax.nn.softmax((h @ router_w).astype(jnp.float32), axis=-1)
    sel = (probs + router_b).reshape(bsz, t_len, N_GROUPS, EXPERTS_PER_GROUP)
    group_score = lax.top_k(sel, GROUP_SCORE_TOPK)[0].sum(-1)
    g_sel = jnp.argmax(group_score, axis=-1)
    in_group = jnp.take_along_axis(sel, g_sel[..., None, None], axis=2)[..., 0, :]
    _, local = lax.top_k(in_group, TOP_K)
    idx = g_sel[..., None] * EXPERTS_PER_GROUP + local
    w_sel = jnp.take_along_axis(probs, idx, axis=-1)
    w_sel = w_sel / jnp.sum(w_sel, -1, keepdims=True)
    combine = jnp.sum(jax.nn.one_hot(idx, N_EXPERTS, dtype=jnp.float32) * w_sel[..., None], axis=-2)
    combine = combine.astype(h.dtype)
    y = jnp.zeros_like(h)
    for e in range(N_EXPERTS):
        he = jax.nn.silu(h @ w_g[e]) * (h @ w_u[e])
        y = y + combine[..., e:e + 1] * (he @ w_d[e])
    return y


def setup_inputs(seed: int = 0) -> dict:
    key = jax.random.key(seed)
    ks = jax.random.split(key, 64)
    counter = [0]

    def nk():
        counter[0] += 1
        return ks[counter[0]]

    def nrm(shape, scale):
        return scale * jax.random.normal(nk(), shape, jnp.float32)

    gate_offset = jnp.zeros((N_ADA, 1), jnp.float32).at[jnp.array([2, 5])].set(1.0)
    return {
        "x": nrm((BATCH, SEQ, D_MODEL), 1.0),
        "c": nrm((BATCH, D_MODEL), 1.0),
        "ada_w": nrm((DEPTH, D_MODEL, N_ADA * D_MODEL), 0.1 * D_MODEL ** -0.5),
        "ada_b": nrm((DEPTH, N_ADA, D_MODEL), 0.02) + gate_offset[None],
        "w_in": nrm((DEPTH, D_MODEL, N_IN_COLS), D_MODEL ** -0.5),
        "gla_alpha_up": nrm((DEPTH, GLA_GATE_RANK, GLA_KW), GLA_GATE_RANK ** -0.5),
        "gla_alpha_b": nrm((DEPTH, GLA_KW), 0.5),
        "gla_norm_g": 1.0 + nrm((DEPTH, GLA_DV), 0.05),
        "rwkv_mu": jax.random.uniform(nk(), (DEPTH, RWKV_COLS), jnp.float32),
        "rwkv_w0": jax.random.uniform(nk(), (DEPTH, BRANCH_W), jnp.float32, -6.0, -1.0),
        "rwkv_w2": nrm((DEPTH, RWKV_W_LORA, BRANCH_W), 0.1 * RWKV_W_LORA ** -0.5),
        "rwkv_a0": nrm((DEPTH, BRANCH_W), 0.3),
        "rwkv_a2": nrm((DEPTH, RWKV_A_LORA, BRANCH_W), RWKV_A_LORA ** -0.5),
        "rwkv_g2": nrm((DEPTH, RWKV_G_LORA, BRANCH_W), RWKV_G_LORA ** -0.5),
        "rwkv_k_k": 0.85 + nrm((DEPTH, BRANCH_W), 0.05),
        "rwkv_k_a": 1.0 + nrm((DEPTH, BRANCH_W), 0.05),
        "rwkv_r_k": nrm((DEPTH, RWKV_HEADS, RWKV_HEAD), 0.1),
        "rwkv_ln_g": 1.0 + nrm((DEPTH, BRANCH_W), 0.05),
        "rwkv_ln_b": nrm((DEPTH, BRANCH_W), 0.02),
        "fox_f_bias": 2.0 + nrm((DEPTH, FOX_HEADS), 0.5),
        "hgrn_lb_logits": nrm((DEPTH, HGRN_KW), 1.0),
        "hgrn_norm_g": 1.0 + nrm((DEPTH, HGRN_DV), 0.05),
        "w_br": nrm((DEPTH, N_BRANCH, BRANCH_W, D_MODEL), DN_BETA * BRANCH_W ** -0.5),
        "w_gate": nrm((DEPTH, N_BRANCH, D_MODEL, D_MODEL), D_MODEL ** -0.5),
        "b_gate": nrm((DEPTH, N_BRANCH, D_MODEL), 0.1),
        "w_o": nrm((DEPTH, D_MODEL, D_MODEL), DN_BETA * D_MODEL ** -0.5),
        "ln1_g": 1.0 + nrm((DEPTH, D_MODEL), 0.05),
        "ln1_b": nrm((DEPTH, D_MODEL), 0.02),
        "router_w": nrm((D_MODEL, N_EXPERTS), D_MODEL ** -0.5),
        "router_b": nrm((N_EXPERTS,), 0.01),
        "exp_w_gate": nrm((DEPTH, N_EXPERTS, D_MODEL, D_FF_EXPERT), D_MODEL ** -0.5),
        "exp_w_up": nrm((DEPTH, N_EXPERTS, D_MODEL, D_FF_EXPERT), DN_BETA * D_MODEL ** -0.5),
        "exp_w_down": nrm((DEPTH, N_EXPERTS, D_FF_EXPERT, D_MODEL), DN_BETA * D_FF_EXPERT ** -0.5),
        "ln2_g": 1.0 + nrm((DEPTH, D_MODEL), 0.05),
        "ln2_b": nrm((DEPTH, D_MODEL), 0.02),
    }


def reference(x, c, ada_w, ada_b, w_in, gla_alpha_up, gla_alpha_b, gla_norm_g,
              rwkv_mu, rwkv_w0, rwkv_w2, rwkv_a0, rwkv_a2, rwkv_g2, rwkv_k_k, rwkv_k_a,
              rwkv_r_k, rwkv_ln_g, rwkv_ln_b, fox_f_bias, hgrn_lb_logits, hgrn_norm_g,
              w_br, w_gate, b_gate, w_o, ln1_g, ln1_b, router_w, router_b,
              exp_w_gate, exp_w_up, exp_w_down, ln2_g, ln2_b):
    bsz = x.shape[0]
    p_lb = jax.nn.softmax(hgrn_lb_logits.astype(jnp.float32), axis=0)
    lb_all = jnp.cumsum(p_lb, axis=0) - p_lb[0]
    cond = jax.nn.silu(c)
    for i in range(DEPTH):
        mod = (cond @ ada_w[i]).reshape(bsz, N_ADA, D_MODEL) + ada_b[i]
        sh1, sc1, gt1, sh2, sc2, gt2 = (mod[:, j, None, :] for j in range(N_ADA))

        h = _layer_norm(x) * (1 + sc1) + sh1
        proj = h @ w_in[i]
        a_cols, b_cols, c_cols, d_cols = _split(proj, [GLA_COLS, RWKV_COLS, FOX_COLS, HGRN_COLS])
        branches = (
            gla_branch(a_cols, gla_alpha_up[i], gla_alpha_b[i], gla_norm_g[i]),
            rwkv7_branch(b_cols, rwkv_mu[i], rwkv_w0[i], rwkv_w2[i], rwkv_a0[i], rwkv_a2[i],
                         rwkv_g2[i], rwkv_k_k[i], rwkv_k_a[i], rwkv_r_k[i], rwkv_ln_g[i], rwkv_ln_b[i]),
            fox_branch(c_cols, fox_f_bias[i]),
            hgrn2_branch(d_cols, lb_all[i], hgrn_norm_g[i]),
        )
        merged = sum(jax.nn.sigmoid(h @ w_gate[i, n] + b_gate[i, n]) * (br @ w_br[i, n])
                     for n, br in enumerate(branches))
        y = merged @ w_o[i]
        x = _layer_norm(DN_ALPHA * x + gt1 * y, ln1_g[i], ln1_b[i])

        h = _layer_norm(x) * (1 + sc2) + sh2
        y = grouped_moe(h, router_w, router_b, exp_w_gate[i], exp_w_up[i], exp_w_down[i])
        x = _layer_norm(DN_ALPHA * x + gt2 * y, ln2_g[i], ln2_b[i])
    return x
```

```python
import functools

import jax
import jax.numpy as jnp
from jax import lax
from jax.experimental import pallas as pl
from jax.experimental.pallas import tpu as pltpu

F32 = jnp.float32
BF16 = jnp.bfloat16
HI = lax.Precision.HIGHEST

D_MODEL = 1024
DEPTH = 2
N_ADA = 6
BRANCH_W = 512
GLA_HEADS = 4
GLA_DK = 64
GLA_KW = 256
GLA_GATE_RANK = 16
GLA_LOGIT_NORM = 16.0
RWKV_HEAD = 64
RWKV_COLS = 1792
RWKV_GN_EPS = 64e-5
FOX_HEADS = 8
FOX_DH = 64
HGRN_KW = 512
N_EXPERTS = 16
D_FF_EXPERT = 512
DN_ALPHA = (2.0 * DEPTH) ** 0.25
GLA_COLS = 1552
FOX_COLS = 1544
HGRN_COLS = 2048
PAD_COLS = 1664

LANES = 128
CHUNK = 64
SUB = 16
VMEM_LIMIT = 56 * 1024 * 1024


def _cparams(sem):
    return pltpu.CompilerParams(dimension_semantics=sem, vmem_limit_bytes=VMEM_LIMIT)


def _sigmoid(x):
    return 1.0 / (1.0 + jnp.exp(-x))


def _log_sigmoid(x):
    return jnp.minimum(x, 0.0) - jnp.log(1.0 + jnp.exp(-jnp.abs(x)))


def _ln(x, eps=1e-5):
    mu = jnp.mean(x, axis=-1, keepdims=True)
    xc = x - mu
    var = jnp.mean(xc * xc, axis=-1, keepdims=True)
    return xc * lax.rsqrt(var + eps)


def _tri(n, strict=False):
    r = lax.broadcasted_iota(jnp.int32, (n, n), 0)
    c = lax.broadcasted_iota(jnp.int32, (n, n), 1)
    return jnp.where((c < r) if strict else (c <= r), 1.0, 0.0).astype(F32)


def _dot(a, b, prec=None):
    return jnp.dot(a, b, preferred_element_type=F32, precision=prec)


def _dot_nt(a, b, prec=None):
    return lax.dot_general(a, b, (((1,), (1,)), ((), ())), preferred_element_type=F32, precision=prec)


def _dot_tn(a, b, prec=None):
    return lax.dot_general(a, b, (((0,), (0,)), ((), ())), preferred_element_type=F32, precision=prec)


def _ada_kernel(c_ref, w_ref, b_ref, o_ref):
    c = c_ref[...]
    cond = c * _sigmoid(c)
    o_ref[0] = _dot(cond, w_ref[0], HI) + b_ref[0]


def _ada_call(c, ada_w, ada_b):
    bsz = c.shape[0]
    n = N_ADA * D_MODEL
    bn = 1536
    out = pl.pallas_call(
        _ada_kernel,
        grid=(DEPTH, n // bn),
        in_specs=[
            pl.BlockSpec((bsz, D_MODEL), lambda i, j: (0, 0)),
            pl.BlockSpec((1, D_MODEL, bn), lambda i, j: (i, 0, j)),
            pl.BlockSpec((1, 1, bn), lambda i, j: (i, 0, j)),
        ],
        out_specs=pl.BlockSpec((1, bsz, bn), lambda i, j: (i, 0, j)),
        out_shape=jax.ShapeDtypeStruct((DEPTH, bsz, n), F32),
        compiler_params=_cparams(("parallel", "parallel")),
        name="ada_mod",
    )(c, ada_w, ada_b.reshape(DEPTH, 1, n))
    return out.reshape(DEPTH, bsz, N_ADA, D_MODEL)


def _inproj_kernel(x_ref, mod_ref, wa_ref, wb_ref, wc_ref, wd_ref, h_ref, oa_ref, ob_ref, oc_ref, od_ref):
    x = x_ref[0]
    sh = mod_ref[0, 0:1, :]
    sc = mod_ref[0, 1:2, :]
    hb = (_ln(x) * (1.0 + sc) + sh).astype(BF16)
    h_ref[0] = hb
    oa_ref[0] = _dot(hb, wa_ref[...])
    ob_ref[0] = _dot(hb, wb_ref[...])
    oc_ref[0] = _dot(hb, wc_ref[...])
    od_ref[0] = _dot(hb, wd_ref[...])


def _inproj_call(x, mod, wa, wb, wc, wd, tm=256):
    bsz, t_len, _ = x.shape
    widths = (wa.shape[1], wb.shape[1], wc.shape[1], wd.shape[1])

    def wspec(w):
        return pl.BlockSpec((D_MODEL, w), lambda b, t: (0, 0), pipeline_mode=pl.Buffered(1))

    def ospec(w):
        return pl.BlockSpec((1, tm, w), lambda b, t: (b, t, 0))

    return pl.pallas_call(
        _inproj_kernel,
        grid=(bsz, t_len // tm),
        in_specs=[pl.BlockSpec((1, tm, D_MODEL), lambda b, t: (b, t, 0)),
                  pl.BlockSpec((1, N_ADA, D_MODEL), lambda b, t: (b, 0, 0))] + [wspec(w) for w in widths],
        out_specs=[ospec(D_MODEL)] + [ospec(w) for w in widths],
        out_shape=[jax.ShapeDtypeStruct((bsz, t_len, D_MODEL), BF16)]
        + [jax.ShapeDtypeStruct((bsz, t_len, w), F32) for w in widths],
        compiler_params=_cparams(("parallel", "parallel")),
        name="inproj",
    )(x, mod, wa, wb, wc, wd)


def _gla_kernel(*refs, mode, layer, hp, qscale, tm):
    if mode == "gla":
        q_ref, k_ref, v_ref, gate_ref, al_ref, aup_ref, ab_ref, ng_ref, sel_ref, o_ref, st_ref = refs
    else:
        q_ref, k_ref, v_ref, gate_ref, lbl_ref, ng_ref, sel_ref, o_ref, st_ref = refs
    c = CHUNK
    nb = c // SUB
    dk = LANES // hp

    @pl.when(pl.program_id(2) == 0)
    def _():
        st_ref[...] = jnp.zeros_like(st_ref)

    lane = lax.broadcasted_iota(jnp.int32, (1, LANES), 1)
    hmask = [((lane >= h * dk) & (lane < (h + 1) * dk)) for h in range(hp)]
    ltri = _tri(c)
    row = lax.broadcasted_iota(jnp.int32, (c, c), 0)
    col = lax.broadcasted_iota(jnp.int32, (c, c), 1)
    off_mask = (col // SUB) < (row // SUB)
    diag_mask = ((col // SUB) == (row // SUB)) & (col <= row)
    ng = ng_ref[...]

    if mode == "hgrn":
        lg = lbl_ref[...]
        mx = jnp.max(lg, axis=0, keepdims=True)
        ex = jnp.exp(lg - mx)
        p = ex / jnp.sum(ex, axis=0, keepdims=True)
        cum = p[0:1, :]
        for j in range(1, layer + 1):
            cum = cum + p[j:j + 1, :]
        lb = cum - p[0:1, :]

    def chunk(ci, carry):
        r0 = pl.multiple_of(ci * c, c)
        q = q_ref[0, pl.ds(r0, c), :]
        if qscale != 1.0:
            q = q * qscale
        kin = k_ref[0, pl.ds(r0, c), :]
        if mode == "gla":
            al = al_ref[0, pl.ds(r0, c), :]
            g = _log_sigmoid(_dot(al, aup_ref[...], HI) + ab_ref[...]) / GLA_LOGIT_NORM
            k = kin
        else:
            sg = _sigmoid(kin)
            g = jnp.log(lb + (1.0 - lb) * sg)
            k = (1.0 - lb) * _sigmoid(-kin)
        v = v_ref[0, pl.ds(r0, c), :]
        b = _dot(ltri, g, HI)
        b_last = b[c - 1:c, :]
        st = st_ref[...]
        stb = st.astype(BF16)
        qe = q * jnp.exp(b)

        rblk = jnp.concatenate([jnp.broadcast_to(b[m * SUB:m * SUB + 1, :], (SUB, LANES)) for m in range(nb)], axis=0)
        qh = q * jnp.exp(b - rblk)
        khat = [None] + [(k * jnp.exp(jnp.minimum(b[m * SUB:m * SUB + 1, :] - b, 0.0))).astype(BF16)
                         for m in range(1, nb)]

        parts = []
        for jj in range(SUB):
            kb = jnp.concatenate([jnp.broadcast_to(k[m * SUB + jj:m * SUB + jj + 1, :], (SUB, LANES))
                                  for m in range(nb)], axis=0)
            bb = jnp.concatenate([jnp.broadcast_to(b[m * SUB + jj:m * SUB + jj + 1, :], (SUB, LANES))
                                  for m in range(nb)], axis=0)
            parts.append((q * kb * jnp.exp(jnp.minimum(b - bb, 0.0))).astype(BF16))
        pw = jnp.concatenate(parts, axis=1)

        kd = (k * jnp.exp(b_last - b)).astype(BF16)
        new_st = st * jnp.exp(b_last)
        for h in range(hp):
            vh = v[:, h * LANES:(h + 1) * LANES].astype(BF16)
            o = _dot_nt(jnp.where(hmask[h], qe, 0.0).astype(BF16), stb)
            qhh = jnp.where(hmask[h], qh, 0.0).astype(BF16)
            a_off = jnp.concatenate(
                [jnp.zeros((SUB, c), F32)] + [_dot_nt(qhh[m * SUB:(m + 1) * SUB, :], khat[m]) for m in range(1, nb)],
                axis=0)
            a_diag = _dot(pw, sel_ref[h])
            a = jnp.where(off_mask, a_off, jnp.where(diag_mask, a_diag, 0.0))
            o = o + _dot(a.astype(BF16), vh)
            o = o * lax.rsqrt(jnp.mean(o * o, axis=-1, keepdims=True) + 1e-6) * ng
            gt = gate_ref[0, pl.ds(r0, c), h * LANES:(h + 1) * LANES]
            o_ref[0, pl.ds(r0, c), h * LANES:(h + 1) * LANES] = o * (gt * _sigmoid(gt))
            new_st = new_st + jnp.where(hmask[h], _dot_tn(vh, kd), 0.0)
        st_ref[...] = new_st
        return carry

    lax.fori_loop(0, tm // c, chunk, 0)


def _pair_select(hp):
    dk = LANES // hp
    r = jnp.arange(SUB * LANES)
    jj = r // LANES
    head = (r % LANES) // dk
    cj = jnp.arange(CHUNK) % SUB
    sel = (jj[None, :, None] == cj[None, None, :]) & (head[None, :, None] == jnp.arange(hp)[:, None, None])
    return sel.astype(BF16)


def _gla_call(proj, alpha_up, alpha_b, norm_g, tm=512):
    bsz, t_len, _ = proj.shape
    hp = 2
    ngroups = GLA_HEADS // hp
    aup = jnp.zeros((LANES, GLA_KW), F32).at[:GLA_GATE_RANK].set(alpha_up)
    lane_blk = lambda off: (lambda b, g, t: (b, t, off + g))
    kern = functools.partial(_gla_kernel, mode="gla", layer=0, hp=hp, qscale=GLA_DK ** -0.5, tm=tm)
    return pl.pallas_call(
        kern,
        grid=(bsz, ngroups, t_len // tm),
        in_specs=[
            pl.BlockSpec((1, tm, LANES), lane_blk(0)),
            pl.BlockSpec((1, tm, LANES), lane_blk(2)),
            pl.BlockSpec((1, tm, hp * LANES), lane_blk(2)),
            pl.BlockSpec((1, tm, hp * LANES), lane_blk(4)),
            pl.BlockSpec((1, tm, LANES), lambda b, g, t: (b, t, 12)),
            pl.BlockSpec((LANES, LANES), lambda b, g, t: (0, g)),
            pl.BlockSpec((1, LANES), lambda b, g, t: (0, g)),
            pl.BlockSpec((1, LANES), lambda b, g, t: (0, 0)),
            pl.BlockSpec((hp, SUB * LANES, CHUNK), lambda b, g, t: (0, 0, 0)),
        ],
        out_specs=pl.BlockSpec((1, tm, hp * LANES), lambda b, g, t: (b, t, g)),
        out_shape=jax.ShapeDtypeStruct((bsz, t_len, BRANCH_W), F32),
        scratch_shapes=[pltpu.VMEM((LANES, LANES), F32)],
        compiler_params=_cparams(("parallel", "parallel", "arbitrary")),
        name="gla",
    )(proj, proj, proj, proj, proj, aup, alpha_b.reshape(1, GLA_KW), norm_g.reshape(1, LANES), _pair_select(hp))


def _hgrn_call(proj, lb_logits, norm_g, layer, tm=512):
    bsz, t_len, _ = proj.shape
    heads = HGRN_KW // LANES
    lane_blk = lambda off: (lambda b, g, t: (b, t, off + g))
    kern = functools.partial(_gla_kernel, mode="hgrn", layer=layer, hp=1, qscale=1.0, tm=tm)
    return pl.pallas_call(
        kern,
        grid=(bsz, heads, t_len // tm),
        in_specs=[
            pl.BlockSpec((1, tm, LANES), lane_blk(0)),
            pl.BlockSpec((1, tm, LANES), lane_blk(4)),
            pl.BlockSpec((1, tm, LANES), lane_blk(8)),
            pl.BlockSpec((1, tm, LANES), lane_blk(12)),
            pl.BlockSpec((DEPTH, LANES), lambda b, g, t: (0, g)),
            pl.BlockSpec((1, LANES), lambda b, g, t: (0, 0)),
            pl.BlockSpec((1, SUB * LANES, CHUNK), lambda b, g, t: (0, 0, 0)),
        ],
        out_specs=pl.BlockSpec((1, tm, LANES), lambda b, g, t: (b, t, g)),
        out_shape=jax.ShapeDtypeStruct((bsz, t_len, BRANCH_W), F32),
        scratch_shapes=[pltpu.VMEM((LANES, LANES), F32)],
        compiler_params=_cparams(("parallel", "parallel", "arbitrary")),
        name="hgrn",
    )(proj, proj, proj, proj, lb_logits, norm_g.reshape(1, LANES), _pair_select(1))


def _rwkv_kernel(cols_ref, mu_ref, w0_ref, w2_ref, a0_ref, a2_ref, g2_ref, kk_ref, ka_ref, rk_ref, lng_ref, lnb_ref,
                 o_ref, carry_ref, st_ref, r_s, lw_s, k_s, v_s, aa_s, bb_s, y_s, *, tm):
    w = BRANCH_W
    c = CHUNK
    npair = w // LANES

    @pl.when(pl.program_id(1) == 0)
    def _():
        carry_ref[...] = jnp.zeros_like(carry_ref)
        st_ref[...] = jnp.zeros_like(st_ref)

    cols = cols_ref[0]
    rowi = lax.broadcasted_iota(jnp.int32, cols.shape, 0)
    prev = jnp.where(rowi == 0, carry_ref[0:1, :], pltpu.roll(cols, 1, 0))
    carry_ref[0:1, :] = cols[tm - 1:tm, :]
    xs = cols + (prev - cols) * mu_ref[...]
    r = xs[:, 0:w]
    k = xs[:, w:2 * w]
    v = xs[:, 2 * w:3 * w]
    wl = xs[:, 3 * w:3 * w + LANES]
    gl = xs[:, 3 * w + LANES:3 * w + 2 * LANES]
    lane = lax.broadcasted_iota(jnp.int32, (1, LANES), 1)
    lo = lane < RWKV_HEAD
    w_raw = _log_sigmoid(w0_ref[...] + _dot(jnp.tanh(wl), w2_ref[...], HI)) - 0.5
    lw = -jnp.exp(w_raw)
    a = _sigmoid(a0_ref[...] + _dot(wl, a2_ref[...], HI))
    g = _dot(_sigmoid(gl), g2_ref[...], HI)
    ri = lax.broadcasted_iota(jnp.int32, (w, w), 0) // RWKV_HEAD
    ci = lax.broadcasted_iota(jnp.int32, (w, w), 1) // RWKV_HEAD
    bd = jnp.where(ri == ci, 1.0, 0.0).astype(F32)
    kk = k * kk_ref[...]
    kk = kk / jnp.maximum(jnp.sqrt(_dot(kk * kk, bd, HI)), 1e-12)
    k = k * (1.0 + (a - 1.0) * ka_ref[...])
    r_s[...] = r
    lw_s[...] = lw
    k_s[...] = k
    v_s[...] = v
    aa_s[...] = -kk
    bb_s[...] = kk * a

    ltri = _tri(c)
    incl = ltri > 0.5
    strict = _tri(c, strict=True) > 0.5
    eye = (lax.broadcasted_iota(jnp.int32, (c, c), 0) == lax.broadcasted_iota(jnp.int32, (c, c), 1))
    eye = jnp.where(eye, 1.0, 0.0).astype(F32)
    hmask = [lo, jnp.logical_not(lo)]
    vrow = lax.broadcasted_iota(jnp.int32, (LANES, LANES), 0) // RWKV_HEAD
    kcol = lax.broadcasted_iota(jnp.int32, (LANES, LANES), 1) // RWKV_HEAD
    bdmask = vrow == kcol

    def chunk(ci_, carry):
        r0 = pl.multiple_of(ci_ * c, c)
        for p in range(npair):
            ls = slice(p * LANES, (p + 1) * LANES)
            rc = r_s[pl.ds(r0, c), ls]
            lwc = lw_s[pl.ds(r0, c), ls]
            kc = k_s[pl.ds(r0, c), ls]
            vc = v_s[pl.ds(r0, c), ls]
            ac = aa_s[pl.ds(r0, c), ls]
            bc = bb_s[pl.ds(r0, c), ls]
            gam = _dot(ltri, lwc, HI)
            eg = jnp.exp(gam)
            ieg = jnp.exp(-gam)
            rt = rc * eg
            at = ac * jnp.exp(gam - lwc)
            bt = bc * ieg
            kt = kc * ieg
            ra = jnp.concatenate([rt, at], axis=0)
            pt = st_ref[p]
            w1 = jnp.zeros((c, LANES), F32)
            w2m = jnp.zeros((c, LANES), F32)
            arb, ark = [], []
            for h in range(2):
                ram = jnp.where(hmask[h], ra, 0.0)
                xb = _dot_nt(ram, bt, HI)
                xk = _dot_nt(ram, kt, HI)
                arb.append(jnp.where(incl, xb[0:c], 0.0))
                ark.append(jnp.where(incl, xk[0:c], 0.0))
                aab = jnp.where(strict, xb[c:2 * c], 0.0)
                aak = jnp.where(strict, xk[c:2 * c], 0.0)
                xp = aab
                tinv = eye + aab
                for _ in range(5):
                    xp = _dot(xp, xp, HI)
                    tinv = tinv + _dot(tinv, xp, HI)
                w1 = w1 + jnp.where(hmask[h], _dot(tinv, _dot(aak, vc, HI), HI), 0.0)
                w2m = w2m + _dot(tinv, jnp.where(hmask[h], at, 0.0), HI)
            rw = jnp.concatenate([rt, w2m], axis=0)
            rp = _dot_nt(rw, pt, HI)
            u = w1 + rp[c:2 * c]
            y = rp[0:c]
            for h in range(2):
                y = y + jnp.where(hmask[h], _dot(arb[h], u, HI) + _dot(ark[h], vc, HI), 0.0)
            y_s[pl.ds(r0, c), ls] = y
            uv = jnp.concatenate([u, vc], axis=0)
            bk = jnp.concatenate([bt, kt], axis=0)
            upd = _dot_tn(uv, bk, HI)
            st_ref[p] = jnp.where(bdmask, (pt + upd) * eg[c - 1:c, :], 0.0)
        return carry

    lax.fori_loop(0, tm // c, chunk, 0)

    y = y_s[...]
    inv = 1.0 / RWKV_HEAD
    mean = _dot(y, bd, HI) * inv
    yc = y - mean
    var = _dot(yc * yc, bd, HI) * inv
    yn = yc * lax.rsqrt(var + RWKV_GN_EPS) * lng_ref[...] + lnb_ref[...]
    bonus = _dot(r * k * rk_ref[...], bd, HI) * v
    o_ref[0] = (yn + bonus) * g


def _rwkv_call(proj, mu, w0, w2, a0, a2, g2, k_k, k_a, r_k, ln_g, ln_b, tm=256):
    bsz, t_len, _ = proj.shape
    w = BRANCH_W
    w2p = jnp.zeros((LANES, w), F32).at[:64].set(w2)
    a2p = jnp.zeros((LANES, w), F32).at[64:].set(a2)
    vec = lambda n: pl.BlockSpec((1, n), lambda b, t: (0, 0))
    mat = lambda n: pl.BlockSpec((n, w), lambda b, t: (0, 0))
    big = lambda: pltpu.VMEM((tm, w), F32)
    return pl.pallas_call(
        functools.partial(_rwkv_kernel, tm=tm),
        grid=(bsz, t_len // tm),
        in_specs=[pl.BlockSpec((1, tm, RWKV_COLS), lambda b, t: (b, t, 0)),
                  vec(RWKV_COLS), vec(w), mat(LANES), vec(w), mat(LANES), mat(LANES),
                  vec(w), vec(w), vec(w), vec(w), vec(w)],
        out_specs=pl.BlockSpec((1, tm, w), lambda b, t: (b, t, 0)),
        out_shape=jax.ShapeDtypeStruct((bsz, t_len, w), F32),
        scratch_shapes=[pltpu.VMEM((8, RWKV_COLS), F32), pltpu.VMEM((w // LANES, LANES, LANES), F32)]
        + [big() for _ in range(7)],
        compiler_params=_cparams(("parallel", "arbitrary")),
        name="rwkv7",
    )(proj, mu.reshape(1, -1), w0.reshape(1, w), w2p, a0.reshape(1, w), a2p, g2,
      k_k.reshape(1, w), k_a.reshape(1, w), r_k.reshape(1, w), ln_g.reshape(1, w), ln_b.reshape(1, w))


def _foxf_kernel(fl_ref, fb_ref, fcol_ref, frow_ref, carry_ref, *, tm):
    @pl.when(pl.program_id(1) == 0)
    def _():
        carry_ref[...] = jnp.zeros_like(carry_ref)

    lf = _log_sigmoid(fl_ref[0] + fb_ref[...])
    cum = _dot(_tri(tm), lf, HI) + carry_ref[0:1, :]
    carry_ref[0:1, :] = cum[tm - 1:tm, :]
    fcol_ref[0] = cum
    frow_ref[0] = cum.T[0:8, :]


def _foxf_call(proj, f_bias, tm=512):
    bsz, t_len, _ = proj.shape
    fb = jnp.zeros((1, LANES), F32).at[0, :FOX_HEADS].set(f_bias)
    return pl.pallas_call(
        functools.partial(_foxf_kernel, tm=tm),
        grid=(bsz, t_len // tm),
        in_specs=[pl.BlockSpec((1, tm, LANES), lambda b, t: (b, t, 12)),
                  pl.BlockSpec((1, LANES), lambda b, t: (0, 0))],
        out_specs=[pl.BlockSpec((1, tm, LANES), lambda b, t: (b, t, 0)),
                   pl.BlockSpec((1, 8, tm), lambda b, t: (b, 0, t))],
        out_shape=[jax.ShapeDtypeStruct((bsz, t_len, LANES), F32),
                   jax.ShapeDtypeStruct((bsz, 8, t_len), F32)],
        scratch_shapes=[pltpu.VMEM((8, LANES), F32)],
        compiler_params=_cparams(("parallel", "arbitrary")),
        name="fox_cumf",
    )(proj, fb)


def _fox_kernel(qi_ref, ki_ref, q_ref, k_ref, v_ref, fcol_ref, frow_ref, o_ref, m_s, l_s, acc_s, *, tq):
    s_idx = pl.program_id(2)
    pair = pl.program_id(1)
    qi = qi_ref[s_idx]
    ki = ki_ref[s_idx]

    @pl.when(ki == 0)
    def _():
        m_s[...] = jnp.full_like(m_s, -jnp.inf)
        l_s[...] = jnp.zeros_like(l_s)
        acc_s[...] = jnp.zeros_like(acc_s)

    lane = lax.broadcasted_iota(jnp.int32, (1, LANES), 1)
    q = q_ref[0] * (FOX_DH ** -0.5)
    kb = k_ref[0].astype(BF16)
    vb = v_ref[0].astype(BF16)
    fcol = fcol_ref[0]
    frow = frow_ref[0]
    row = lax.broadcasted_iota(jnp.int32, (tq, tq), 0)
    col = lax.broadcasted_iota(jnp.int32, (tq, tq), 1)
    keep = col <= row + jnp.where(ki < qi, tq, 0)
    for h in range(2):
        hm = (lane >= h * FOX_DH) & (lane < (h + 1) * FOX_DH)
        hsel = lane == (2 * pair + h)
        fq = jnp.sum(jnp.where(hsel, fcol, 0.0), axis=-1, keepdims=True)
        rsel = lax.broadcasted_iota(jnp.int32, (8, 1), 0) == (2 * pair + h)
        fk = jnp.sum(jnp.where(rsel, frow, 0.0), axis=0, keepdims=True)
        s = _dot_nt(jnp.where(hm, q, 0.0).astype(BF16), kb) + (fq - fk)
        s = jnp.where(keep, s, -1e30)
        m_old = m_s[h]
        m_new = jnp.maximum(m_old, jnp.max(s, axis=-1, keepdims=True))
        alpha = jnp.exp(m_old - m_new)
        p = jnp.exp(s - m_new)
        l_s[h] = alpha * l_s[h] + jnp.sum(p, axis=-1, keepdims=True)
        acc_s[h] = alpha * acc_s[h] + _dot(p.astype(BF16), vb)
        m_s[h] = m_new

    @pl.when(ki == qi)
    def _():
        o0 = acc_s[0] / l_s[0]
        o1 = acc_s[1] / l_s[1]
        o_ref[0] = jnp.where(lane < FOX_DH, o0, o1)


def _fox_call(proj, fcol, frow, tq=512):
    bsz, t_len, _ = proj.shape
    nq = t_len // tq
    pairs = [(i, j) for i in range(nq) for j in range(i + 1)]
    qi_tab = jnp.asarray([p[0] for p in pairs], jnp.int32)
    ki_tab = jnp.asarray([p[1] for p in pairs], jnp.int32)
    npairs = FOX_HEADS // 2
    grid_spec = pltpu.PrefetchScalarGridSpec(
        num_scalar_prefetch=2,
        grid=(bsz, npairs, len(pairs)),
        in_specs=[
            pl.BlockSpec((1, tq, LANES), lambda b, p, s, qi, ki: (b, qi[s], p)),
            pl.BlockSpec((1, tq, LANES), lambda b, p, s, qi, ki: (b, ki[s], 4 + p)),
            pl.BlockSpec((1, tq, LANES), lambda b, p, s, qi, ki: (b, ki[s], 8 + p)),
            pl.BlockSpec((1, tq, LANES), lambda b, p, s, qi, ki: (b, qi[s], 0)),
            pl.BlockSpec((1, 8, tq), lambda b, p, s, qi, ki: (b, 0, ki[s])),
        ],
        out_specs=pl.BlockSpec((1, tq, LANES), lambda b, p, s, qi, ki: (b, qi[s], p)),
        scratch_shapes=[pltpu.VMEM((2, tq, 1), F32), pltpu.VMEM((2, tq, 1), F32), pltpu.VMEM((2, tq, LANES), F32)],
    )
    return pl.pallas_call(
        functools.partial(_fox_kernel, tq=tq),
        grid_spec=grid_spec,
        out_shape=jax.ShapeDtypeStruct((bsz, t_len, BRANCH_W), F32),
        compiler_params=_cparams(("parallel", "parallel", "arbitrary")),
        name="fox_attn",
    )(qi_tab, ki_tab, proj, proj, proj, fcol, frow)


def _merge_kernel(x_ref, h_ref, ba_ref, bb_ref, bc_ref, bd_ref, mod_ref, wg_ref, bg_ref, wbr_ref, wo_ref,
                  l1g_ref, l1b_ref, rw_ref, rb_ref, x1_ref, h2_ref, cmb_ref):
    hb = h_ref[0]
    merged = None
    for n, br_ref in enumerate((ba_ref, bb_ref, bc_ref, bd_ref)):
        gate = _sigmoid(_dot(hb, wg_ref[n]) + bg_ref[n:n + 1, :])
        term = gate * _dot(br_ref[0].astype(BF16), wbr_ref[n])
        merged = term if merged is None else merged + term
    y = _dot(merged.astype(BF16), wo_ref[...])
    gt1 = mod_ref[0, 2:3, :]
    x1 = _ln(DN_ALPHA * x_ref[0] + gt1 * y) * l1g_ref[...] + l1b_ref[...]
    x1_ref[0] = x1
    h2 = _ln(x1) * (1.0 + mod_ref[0, 4:5, :]) + mod_ref[0, 3:4, :]
    h2_ref[0] = h2.astype(BF16)

    tm = x1.shape[0]
    lane = lax.broadcasted_iota(jnp.int32, (tm, LANES), 1)
    lanef = lane.astype(F32)
    neg = -jnp.inf
    logits = jnp.where(lane < N_EXPERTS, _dot(h2, rw_ref[...], HI), neg)
    ex = jnp.exp(logits - jnp.max(logits, axis=-1, keepdims=True))
    probs = ex / jnp.sum(ex, axis=-1, keepdims=True)
    sel = probs + rb_ref[...]
    grp = lane // 4

    def top2(vals):
        m1 = jnp.max(vals, axis=-1, keepdims=True)
        i1 = jnp.min(jnp.where(vals == m1, lanef, 999.0), axis=-1, keepdims=True)
        rest = jnp.where(lanef == i1, neg, vals)
        m2 = jnp.max(rest, axis=-1, keepdims=True)
        i2 = jnp.min(jnp.where(rest == m2, lanef, 999.0), axis=-1, keepdims=True)
        return m1, i1, m2, i2

    best = None
    for gidx in range(4):
        m1, _, m2, _ = top2(jnp.where(grp == gidx, sel, neg))
        score = m1 + m2
        if best is None:
            best, gsel = score, jnp.zeros_like(score, dtype=jnp.int32)
        else:
            better = score > best
            gsel = jnp.where(better, gidx, gsel)
            best = jnp.where(better, score, best)
    _, i1, _, i2 = top2(jnp.where(grp == gsel, sel, neg))
    w1 = jnp.sum(jnp.where(lanef == i1, probs, 0.0), axis=-1, keepdims=True)
    w2 = jnp.sum(jnp.where(lanef == i2, probs, 0.0), axis=-1, keepdims=True)
    cmb_ref[0] = (jnp.where(lanef == i1, w1, 0.0) + jnp.where(lanef == i2, w2, 0.0)) / (w1 + w2)


def _merge_call(x, h, branches, mod, wg, bg, wbr, wo, l1g, l1b, rw, rb, tm=256):
    bsz, t_len, _ = x.shape
    tok = lambda w: pl.BlockSpec((1, tm, w), lambda b, t: (b, t, 0))
    const = lambda shape: pl.BlockSpec(shape, lambda b, t: (0,) * len(shape), pipeline_mode=pl.Buffered(1))
    return pl.pallas_call(
        _merge_kernel,
        grid=(bsz, t_len // tm),
        in_specs=[tok(D_MODEL), tok(D_MODEL)] + [tok(BRANCH_W)] * 4
        + [pl.BlockSpec((1, N_ADA, D_MODEL), lambda b, t: (b, 0, 0)),
           const((4, D_MODEL, D_MODEL)), const((4, D_MODEL)), const((4, BRANCH_W, D_MODEL)), const((D_MODEL, D_MODEL)),
           const((1, D_MODEL)), const((1, D_MODEL)), const((D_MODEL, LANES)), const((1, LANES))],
        out_specs=[tok(D_MODEL), tok(D_MODEL), tok(LANES)],
        out_shape=[jax.ShapeDtypeStruct((bsz, t_len, D_MODEL), F32),
                   jax.ShapeDtypeStruct((bsz, t_len, D_MODEL), BF16),
                   jax.ShapeDtypeStruct((bsz, t_len, LANES), F32)],
        compiler_params=_cparams(("parallel", "parallel")),
        name="merge",
    )(x, h, *branches, mod, wg, bg, wbr, wo, l1g, l1b, rw, rb)


def _moe_kernel(h_ref, cmb_ref, x_ref, mod_ref, wg_ref, wu_ref, wd_ref, l2g_ref, l2b_ref, o_ref, acc_ref):
    e = pl.program_id(2)

    @pl.when(e == 0)
    def _():
        acc_ref[...] = jnp.zeros_like(acc_ref)

    hb = h_ref[0]
    a = _dot(hb, wg_ref[0])
    u = _dot(hb, wu_ref[0])
    he = (a * _sigmoid(a) * u).astype(BF16)
    cmb = cmb_ref[0]
    lane = lax.broadcasted_iota(jnp.int32, cmb.shape, 1)
    ce = jnp.sum(jnp.where(lane == e, cmb, 0.0), axis=-1, keepdims=True)
    acc_ref[...] += ce * _dot(he, wd_ref[0])

    @pl.when(e == N_EXPERTS - 1)
    def _():
        gt2 = mod_ref[0, 5:6, :]
        o_ref[0] = _ln(DN_ALPHA * x_ref[0] + gt2 * acc_ref[...]) * l2g_ref[...] + l2b_ref[...]


def _moe_call(h2, cmb, x1, mod, wg, wu, wd, l2g, l2b, tm=1024):
    bsz, t_len, _ = x1.shape
    tok = lambda w: pl.BlockSpec((1, tm, w), lambda b, t, e: (b, t, 0))
    return pl.pallas_call(
        _moe_kernel,
        grid=(bsz, t_len // tm, N_EXPERTS),
        in_specs=[tok(D_MODEL), tok(LANES), tok(D_MODEL),
                  pl.BlockSpec((1, N_ADA, D_MODEL), lambda b, t, e: (b, 0, 0)),
                  pl.BlockSpec((1, D_MODEL, D_FF_EXPERT), lambda b, t, e: (e, 0, 0)),
                  pl.BlockSpec((1, D_MODEL, D_FF_EXPERT), lambda b, t, e: (e, 0, 0)),
                  pl.BlockSpec((1, D_FF_EXPERT, D_MODEL), lambda b, t, e: (e, 0, 0)),
                  pl.BlockSpec((1, D_MODEL), lambda b, t, e: (0, 0)),
                  pl.BlockSpec((1, D_MODEL), lambda b, t, e: (0, 0))],
        out_specs=tok(D_MODEL),
        out_shape=jax.ShapeDtypeStruct((bsz, t_len, D_MODEL), F32),
        scratch_shapes=[pltpu.VMEM((tm, D_MODEL), F32)],
        compiler_params=_cparams(("parallel", "parallel", "arbitrary")),
        name="moe",
    )(h2, cmb, x1, mod, wg, wu, wd, l2g, l2b)


def _pad_cols(w, n):
    return jnp.pad(w, ((0, 0), (0, n - w.shape[1])))


def kernel(x, c, ada_w, ada_b, w_in, gla_alpha_up, gla_alpha_b, gla_norm_g, rwkv_mu, rwkv_w0, rwkv_w2, rwkv_a0, rwkv_a2, rwkv_g2, rwkv_k_k, rwkv_k_a, rwkv_r_k, rwkv_ln_g, rwkv_ln_b, fox_f_bias, hgrn_lb_logits, hgrn_norm_g, w_br, w_gate, b_gate, w_o, ln1_g, ln1_b, router_w, router_b, exp_w_gate, exp_w_up, exp_w_down, ln2_g, ln2_b):
    mod_all = _ada_call(c, ada_w, ada_b)
    rw = _pad_cols(router_w, LANES)
    rb = _pad_cols(router_b.reshape(1, N_EXPERTS), LANES)
    o1, o2, o3 = GLA_COLS, GLA_COLS + RWKV_COLS, GLA_COLS + RWKV_COLS + FOX_COLS
    for i in range(DEPTH):
        mod = mod_all[i]
        wi = w_in[i]
        wa = _pad_cols(wi[:, :o1], PAD_COLS).astype(BF16)
        wb = wi[:, o1:o2].astype(BF16)
        wc = _pad_cols(wi[:, o2:o3], PAD_COLS).astype(BF16)
        wd = wi[:, o3:].astype(BF16)
        h, pa, pb, pc, pd = _inproj_call(x, mod, wa, wb, wc, wd)
        br_a = _gla_call(pa, gla_alpha_up[i], gla_alpha_b[i], gla_norm_g[i])
        br_b = _rwkv_call(pb, rwkv_mu[i], rwkv_w0[i], rwkv_w2[i], rwkv_a0[i], rwkv_a2[i], rwkv_g2[i],
                          rwkv_k_k[i], rwkv_k_a[i], rwkv_r_k[i], rwkv_ln_g[i], rwkv_ln_b[i])
        fcol, frow = _foxf_call(pc, fox_f_bias[i])
        br_c = _fox_call(pc, fcol, frow)
        br_d = _hgrn_call(pd, hgrn_lb_logits, hgrn_norm_g[i], i)
        x1, h2, cmb = _merge_call(
            x, h, (br_a, br_b, br_c, br_d), mod, w_gate[i].astype(BF16), b_gate[i], w_br[i].astype(BF16),
            w_o[i].astype(BF16), ln1_g[i].reshape(1, -1), ln1_b[i].reshape(1, -1), rw, rb)
        x = _moe_call(h2, cmb, x1, mod, exp_w_gate[i].astype(BF16), exp_w_up[i].astype(BF16),
                      exp_w_down[i].astype(BF16), ln2_g[i].reshape(1, -1), ln2_b[i].reshape(1, -1))
    return x
```

```python
import functools

import jax
import jax.numpy as jnp
from jax import lax
from jax.experimental import pallas as pl
from jax.experimental.pallas import tpu as pltpu

F32 = jnp.float32
BF16 = jnp.bfloat16
HI = lax.Precision.HIGHEST

D_MODEL = 1024
DEPTH = 2
N_ADA = 6
BRANCH_W = 512
GLA_HEADS = 4
GLA_DK = 64
GLA_KW = 256
GLA_GATE_RANK = 16
GLA_LOGIT_NORM = 16.0
RWKV_HEAD = 64
RWKV_COLS = 1792
RWKV_GN_EPS = 64e-5
FOX_HEADS = 8
FOX_DH = 64
HGRN_KW = 512
N_EXPERTS = 16
D_FF_EXPERT = 512
DN_ALPHA = (2.0 * DEPTH) ** 0.25
GLA_COLS = 1552
FOX_COLS = 1544
HGRN_COLS = 2048
PAD_COLS = 1664

LANES = 128
CHUNK = 64
SUB = 16
VMEM_LIMIT = 56 * 1024 * 1024


def _cparams(sem):
    return pltpu.CompilerParams(dimension_semantics=sem, vmem_limit_bytes=VMEM_LIMIT)


def _sigmoid(x):
    return 1.0 / (1.0 + jnp.exp(-x))


def _log_sigmoid(x):
    return jnp.minimum(x, 0.0) - jnp.log(1.0 + jnp.exp(-jnp.abs(x)))


def _ln(x, eps=1e-5):
    mu = jnp.mean(x, axis=-1, keepdims=True)
    xc = x - mu
    var = jnp.mean(xc * xc, axis=-1, keepdims=True)
    return xc * lax.rsqrt(var + eps)


def _tri(n, strict=False):
    r = lax.broadcasted_iota(jnp.int32, (n, n), 0)
    c = lax.broadcasted_iota(jnp.int32, (n, n), 1)
    return jnp.where((c < r) if strict else (c <= r), 1.0, 0.0).astype(F32)


def _dot(a, b, prec=None):
    return jnp.dot(a, b, preferred_element_type=F32, precision=prec)


def _dot_nt(a, b, prec=None):
    return lax.dot_general(a, b, (((1,), (1,)), ((), ())), preferred_element_type=F32, precision=prec)


def _dot_tn(a, b, prec=None):
    return lax.dot_general(a, b, (((0,), (0,)), ((), ())), preferred_element_type=F32, precision=prec)


def _ada_kernel(c_ref, w_ref, b_ref, o_ref):
    c = c_ref[...]
    cond = c * _sigmoid(c)
    o_ref[0] = _dot(cond, w_ref[0], HI) + b_ref[0]


def _ada_call(c, ada_w, ada_b):
    bsz = c.shape[0]
    n = N_ADA * D_MODEL
    bn = 1536
    out = pl.pallas_call(
        _ada_kernel,
        grid=(DEPTH, n // bn),
        in_specs=[
            pl.BlockSpec((bsz, D_MODEL), lambda i, j: (0, 0)),
            pl.BlockSpec((1, D_MODEL, bn), lambda i, j: (i, 0, j)),
            pl.BlockSpec((1, 1, bn), lambda i, j: (i, 0, j)),
        ],
        out_specs=pl.BlockSpec((1, bsz, bn), lambda i, j: (i, 0, j)),
        out_shape=jax.ShapeDtypeStruct((DEPTH, bsz, n), F32),
        compiler_params=_cparams(("parallel", "parallel")),
        name="ada_mod",
    )(c, ada_w, ada_b.reshape(DEPTH, 1, n))
    return out.reshape(DEPTH, bsz, N_ADA, D_MODEL)


def _inproj_kernel(x_ref, mod_ref, wa_ref, wb_ref, wc_ref, wd_ref, h_ref, oa_ref, ob_ref, oc_ref, od_ref):
    x = x_ref[0]
    sh = mod_ref[0, 0:1, :]
    sc = mod_ref[0, 1:2, :]
    hb = (_ln(x) * (1.0 + sc) + sh).astype(BF16)
    h_ref[0] = hb
    oa_ref[0] = _dot(hb, wa_ref[...])
    ob_ref[0] = _dot(hb, wb_ref[...])
    oc_ref[0] = _dot(hb, wc_ref[...])
    od_ref[0] = _dot(hb, wd_ref[...])


def _inproj_call(x, mod, wa, wb, wc, wd, tm=256):
    bsz, t_len, _ = x.shape
    widths = (wa.shape[1], wb.shape[1], wc.shape[1], wd.shape[1])

    def wspec(w):
        return pl.BlockSpec((D_MODEL, w), lambda b, t: (0, 0), pipeline_mode=pl.Buffered(1))

    def ospec(w):
        return pl.BlockSpec((1, tm, w), lambda b, t: (b, t, 0))

    return pl.pallas_call(
        _inproj_kernel,
        grid=(bsz, t_len // tm),
        in_specs=[pl.BlockSpec((1, tm, D_MODEL), lambda b, t: (b, t, 0)),
                  pl.BlockSpec((1, N_ADA, D_MODEL), lambda b, t: (b, 0, 0))] + [wspec(w) for w in widths],
        out_specs=[ospec(D_MODEL)] + [ospec(w) for w in widths],
        out_shape=[jax.ShapeDtypeStruct((bsz, t_len, D_MODEL), BF16)]
        + [jax.ShapeDtypeStruct((bsz, t_len, w), F32) for w in widths],
        compiler_params=_cparams(("parallel", "parallel")),
        name="inproj",
    )(x, mod, wa, wb, wc, wd)


def _gla_kernel(*refs, mode, layer, hp, qscale, tm, ngrp):
    if mode == "gla":
        q_ref, k_ref, v_ref, gate_ref, al_ref, aup_ref, ab_ref, ng_ref, sel_ref, o_ref = refs[:10]
    else:
        q_ref, k_ref, v_ref, gate_ref, lbl_ref, ng_ref, sel_ref, o_ref = refs[:8]
    st_ref, q_s, k_s, b_s, qe_s, kd_s, pw_s = refs[-7:]
    c = CHUNK
    nb = c // SUB
    dk = LANES // hp

    @pl.when(pl.program_id(1) == 0)
    def _():
        st_ref[...] = jnp.zeros_like(st_ref)

    q = q_ref[0]
    if qscale != 1.0:
        q = q * qscale
    kin = k_ref[0]
    if mode == "gla":
        g = _log_sigmoid(_dot(al_ref[0], aup_ref[...], HI) + ab_ref[...]) / GLA_LOGIT_NORM
        k = kin
    else:
        lg = lbl_ref[...]
        mx = jnp.max(lg, axis=0, keepdims=True)
        ex = jnp.exp(lg - mx)
        p = ex / jnp.sum(ex, axis=0, keepdims=True)
        cum = p[0:1, :]
        for j in range(1, layer + 1):
            cum = cum + p[j:j + 1, :]
        lb = cum - p[0:1, :]
        g = jnp.log(lb + (1.0 - lb) * _sigmoid(kin))
        k = (1.0 - lb) * _sigmoid(-kin)
    tr = lax.broadcasted_iota(jnp.int32, (tm, tm), 0)
    tc = lax.broadcasted_iota(jnp.int32, (tm, tm), 1)
    same = (tr // c) == (tc // c)
    b = _dot_split_rhs(jnp.where((tc <= tr) & same, 1.0, 0.0).astype(BF16), g, 3)
    rev = _dot_split_rhs(jnp.where((tc > tr) & same, 1.0, 0.0).astype(BF16), g, 3)
    q_s[...] = q
    k_s[...] = k
    b_s[...] = b
    qe_s[...] = (q * jnp.exp(b)).astype(BF16)
    kd_s[...] = (k * jnp.exp(rev)).astype(BF16)

    lane = lax.broadcasted_iota(jnp.int32, (1, LANES), 1)
    hmask = [((lane >= h * dk) & (lane < (h + 1) * dk)) for h in range(hp)]
    row = lax.broadcasted_iota(jnp.int32, (c, c), 0)
    col = lax.broadcasted_iota(jnp.int32, (c, c), 1)
    off_mask = (col // SUB) < (row // SUB)
    diag_mask = ((col // SUB) == (row // SUB)) & (col <= row)
    ng = ng_ref[...]

    def head_lanes(x, h):
        return x if hp == 1 else jnp.where(hmask[h], x, jnp.zeros_like(x))

    def chunk(ci, carry):
        r0 = pl.multiple_of(ci * c, c)
        rows = pl.ds(r0, c)
        sts, o_int, upds, vbs = [], {}, {}, {}
        for gi in range(ngrp):
            gl = slice(gi * LANES, (gi + 1) * LANES)
            sts.append(st_ref[gi])
            stb = sts[gi].astype(BF16)
            for h in range(hp):
                hv = gi * hp + h
                vbs[hv] = v_ref[0, rows, hv * LANES:(hv + 1) * LANES].astype(BF16)
                o_int[hv] = _dot_nt(head_lanes(qe_s[rows, gl], h), stb)
                upds[hv] = _dot_tn(vbs[hv], kd_s[rows, gl])
        qhs, khats = {}, []
        for gi in range(ngrp):
            gl = slice(gi * LANES, (gi + 1) * LANES)
            qc = q_s[rows, gl]
            kc = k_s[rows, gl]
            bc = b_s[rows, gl]
            rblk = jnp.concatenate([jnp.broadcast_to(bc[m * SUB:m * SUB + 1, :], (SUB, LANES)) for m in range(nb)],
                                   axis=0)
            qh = (qc * jnp.exp(bc - rblk)).astype(BF16)
            for h in range(hp):
                qhs[gi * hp + h] = head_lanes(qh, h)
            khats.append([None] + [(kc * jnp.exp(jnp.minimum(bc[m * SUB:m * SUB + 1, :] - bc, 0.0))).astype(BF16)
                                   for m in range(1, nb)])
            for jj in range(SUB):
                kb = jnp.concatenate([jnp.broadcast_to(kc[m * SUB + jj:m * SUB + jj + 1, :], (SUB, LANES))
                                      for m in range(nb)], axis=0)
                bb = jnp.concatenate([jnp.broadcast_to(bc[m * SUB + jj:m * SUB + jj + 1, :], (SUB, LANES))
                                      for m in range(nb)], axis=0)
                pw_s[gi * c:(gi + 1) * c, jj * LANES:(jj + 1) * LANES] = (
                    qc * kb * jnp.exp(jnp.minimum(bc - bb, 0.0))).astype(BF16)
        a_offs = {}
        for hv in range(ngrp * hp):
            a_offs[hv] = jnp.concatenate(
                [jnp.zeros((SUB, c), F32)]
                + [_dot_nt(qhs[hv][m * SUB:(m + 1) * SUB, :], khats[hv // hp][m]) for m in range(1, nb)], axis=0)
        a_diags = [_dot(pw_s[...], sel_ref[h]) for h in range(hp)]
        for hv in range(ngrp * hp):
            gi, h = hv // hp, hv % hp
            a = jnp.where(off_mask, a_offs[hv], jnp.where(diag_mask, a_diags[h][gi * c:(gi + 1) * c], 0.0))
            o = o_int[hv] + _dot(a.astype(BF16), vbs[hv])
            o = o * lax.rsqrt(jnp.mean(o * o, axis=-1, keepdims=True) + 1e-6) * ng
            gt = gate_ref[0, rows, hv * LANES:(hv + 1) * LANES]
            o_ref[0, rows, hv * LANES:(hv + 1) * LANES] = o * (gt * _sigmoid(gt))
        for gi in range(ngrp):
            gl = slice(gi * LANES, (gi + 1) * LANES)
            b_last = b_s[pl.ds(pl.multiple_of(r0 + c - 8, 8), 8), gl][7:8, :]
            new_st = sts[gi] * jnp.exp(b_last)
            for h in range(hp):
                new_st = new_st + head_lanes(upds[gi * hp + h], h)
            st_ref[gi] = new_st
        return carry

    lax.fori_loop(0, tm // c, chunk, 0)


def _pair_select(hp):
    dk = LANES // hp
    r = jnp.arange(SUB * LANES)
    jj = r // LANES
    head = (r % LANES) // dk
    cj = jnp.arange(CHUNK) % SUB
    sel = (jj[None, :, None] == cj[None, None, :]) & (head[None, :, None] == jnp.arange(hp)[:, None, None])
    return sel.astype(BF16)


def _gla_scratch(tm, ngrp):
    w = ngrp * LANES
    return ([pltpu.VMEM((ngrp, LANES, LANES), F32)] + [pltpu.VMEM((tm, w), F32) for _ in range(3)]
            + [pltpu.VMEM((tm, w), BF16) for _ in range(2)] + [pltpu.VMEM((ngrp * CHUNK, SUB * LANES), BF16)])


def _gla_call(proj, alpha_up, alpha_b, norm_g, tm=256):
    bsz, t_len, _ = proj.shape
    hp = 2
    ngrp = GLA_HEADS // hp
    aup = jnp.zeros((LANES, GLA_KW), F32).at[:GLA_GATE_RANK].set(alpha_up)
    blk = lambda w, j: pl.BlockSpec((1, tm, w), lambda b, t: (b, t, j))
    const = lambda shape: pl.BlockSpec(shape, lambda b, t: (0,) * len(shape))
    kern = functools.partial(_gla_kernel, mode="gla", layer=0, hp=hp, qscale=GLA_DK ** -0.5, tm=tm, ngrp=ngrp)
    return pl.pallas_call(
        kern,
        grid=(bsz, t_len // tm),
        in_specs=[blk(GLA_KW, 0), blk(GLA_KW, 1), blk(BRANCH_W, 1), blk(BRANCH_W, 2), blk(LANES, 12),
                  const((LANES, GLA_KW)), const((1, GLA_KW)), const((1, LANES)), const((hp, SUB * LANES, CHUNK))],
        out_specs=blk(BRANCH_W, 0),
        out_shape=jax.ShapeDtypeStruct((bsz, t_len, BRANCH_W), F32),
        scratch_shapes=_gla_scratch(tm, ngrp),
        compiler_params=_cparams(("parallel", "arbitrary")),
        name="gla",
    )(proj, proj, proj, proj, proj, aup, alpha_b.reshape(1, GLA_KW), norm_g.reshape(1, LANES), _pair_select(hp))


def _hgrn_call(proj, lb_logits, norm_g, layer, tm=256):
    bsz, t_len, _ = proj.shape
    ngrp = HGRN_KW // LANES
    blk = lambda j: pl.BlockSpec((1, tm, HGRN_KW), lambda b, t: (b, t, j))
    const = lambda shape: pl.BlockSpec(shape, lambda b, t: (0,) * len(shape))
    kern = functools.partial(_gla_kernel, mode="hgrn", layer=layer, hp=1, qscale=1.0, tm=tm, ngrp=ngrp)
    return pl.pallas_call(
        kern,
        grid=(bsz, t_len // tm),
        in_specs=[blk(0), blk(1), blk(2), blk(3), const((DEPTH, HGRN_KW)), const((1, LANES)),
                  const((1, SUB * LANES, CHUNK))],
        out_specs=blk(0),
        out_shape=jax.ShapeDtypeStruct((bsz, t_len, BRANCH_W), F32),
        scratch_shapes=_gla_scratch(tm, ngrp),
        compiler_params=_cparams(("parallel", "arbitrary")),
        name="hgrn",
    )(proj, proj, proj, proj, lb_logits, norm_g.reshape(1, LANES), _pair_select(1))


def _split_bf16(x, n):
    parts, rest = [], x
    for i in range(n):
        part = rest.astype(BF16)
        parts.append(part)
        if i + 1 < n:
            rest = rest - part.astype(F32)
    return parts


def _dot_split_rhs(a01, x, n):
    out = None
    for part in _split_bf16(x, n):
        term = _dot(a01, part)
        out = term if out is None else out + term
    return out


def _dot_split_lhs(x, b01, n):
    out = None
    for part in _split_bf16(x, n):
        term = _dot(part, b01)
        out = term if out is None else out + term
    return out


def _rwkv_kernel(cols_ref, mu_ref, w0_ref, w2_ref, a0_ref, a2_ref, g2_ref, kk_ref, ka_ref, rk_ref, lng_ref, lnb_ref,
                 o_ref, carry_ref, st_ref, rt_s, at_s, bt_s, kt_s, vb_s, w2_s, eg_s, y_s, w1_s, tops_s, *, tm):
    w = BRANCH_W
    c = CHUNK
    npair = w // LANES

    @pl.when(pl.program_id(1) == 0)
    def _():
        carry_ref[...] = jnp.zeros_like(carry_ref)
        st_ref[...] = jnp.zeros_like(st_ref)

    cols = cols_ref[0]
    rowi = lax.broadcasted_iota(jnp.int32, cols.shape, 0)
    prev = jnp.where(rowi == 0, carry_ref[0:1, :], pltpu.roll(cols, 1, 0))
    carry_ref[0:1, :] = cols[tm - 1:tm, :]
    xs = cols + (prev - cols) * mu_ref[...]
    r = xs[:, 0:w]
    k = xs[:, w:2 * w]
    v = xs[:, 2 * w:3 * w]
    wl = xs[:, 3 * w:3 * w + LANES]
    gl = xs[:, 3 * w + LANES:3 * w + 2 * LANES]
    w_raw = _log_sigmoid(w0_ref[...] + _dot(jnp.tanh(wl).astype(BF16), w2_ref[...])) - 0.5
    lw = -jnp.exp(w_raw)
    a = _sigmoid(a0_ref[...] + _dot(wl.astype(BF16), a2_ref[...]))
    g = _dot(_sigmoid(gl).astype(BF16), g2_ref[...])

    hr = lax.broadcasted_iota(jnp.int32, (LANES, LANES), 0) // RWKV_HEAD
    hc = lax.broadcasted_iota(jnp.int32, (LANES, LANES), 1) // RWKV_HEAD
    bdmask = hr == hc
    bd = jnp.where(bdmask, 1.0, 0.0).astype(BF16)

    def head_sum(t, n):
        return jnp.concatenate([_dot_split_lhs(t[:, p * LANES:(p + 1) * LANES], bd, n) for p in range(npair)], axis=1)

    kk = k * kk_ref[...]
    kk = kk / jnp.maximum(jnp.sqrt(head_sum(kk * kk, 3)), 1e-12)
    k = k * (1.0 + (a - 1.0) * ka_ref[...])

    tr = lax.broadcasted_iota(jnp.int32, (tm, tm), 0)
    tc = lax.broadcasted_iota(jnp.int32, (tm, tm), 1)
    tri = jnp.where((tc <= tr) & ((tr // c) == (tc // c)), 1.0, 0.0).astype(BF16)
    gam = _dot_split_rhs(tri, lw, 3)
    eg = jnp.exp(gam)
    ieg = jnp.exp(-gam)
    rt_s[...] = (r * eg).astype(BF16)
    at_s[...] = (-kk * jnp.exp(gam - lw)).astype(BF16)
    bt_s[...] = (kk * a * ieg).astype(BF16)
    kt_s[...] = (k * ieg).astype(BF16)
    vb_s[...] = v.astype(BF16)
    eg_s[...] = eg

    lane = lax.broadcasted_iota(jnp.int32, (1, LANES), 1)
    lo = lane < RWKV_HEAD
    hmask = [lo, jnp.logical_not(lo)]
    row = lax.broadcasted_iota(jnp.int32, (c, LANES), 0)
    col = lax.broadcasted_iota(jnp.int32, (c, LANES), 1)
    top_mask = (col % c) <= row
    ak_mask = (col >= c) & ((col - c) < row)
    r64 = lax.broadcasted_iota(jnp.int32, (c, c), 0)
    c64 = lax.broadcasted_iota(jnp.int32, (c, c), 1)
    strict = c64 < r64
    eye = jnp.where(c64 == r64, 1.0, 0.0).astype(F32)

    def solve(j, carry):
        base = j * (2 * c)
        chains = []
        for cc in range(2):
            r0 = pl.multiple_of(base + cc * c, c)
            for p in range(npair):
                ls = slice(p * LANES, (p + 1) * LANES)
                at = at_s[pl.ds(r0, c), ls]
                vb = vb_s[pl.ds(r0, c), ls]
                ra = jnp.concatenate([rt_s[pl.ds(r0, c), ls], at], axis=0)
                bk = jnp.concatenate([bt_s[pl.ds(r0, c), ls], kt_s[pl.ds(r0, c), ls]], axis=0)
                vv = jnp.concatenate([vb, vb], axis=0)
                for h in range(2):
                    x = _dot_nt(jnp.where(hmask[h], ra, jnp.zeros_like(ra)), bk)
                    tops_s[pl.ds(r0, c), (2 * p + h) * LANES:(2 * p + h + 1) * LANES] = (
                        jnp.where(top_mask, x[0:c], 0.0).astype(BF16))
                    chains.append(dict(r0=r0, p=p, h=h, vv=vv, atm=jnp.where(hmask[h], at, jnp.zeros_like(at)),
                                       aab=jnp.where(strict, x[c:2 * c, 0:c], 0.0),
                                       akm=jnp.where(ak_mask, x[c:2 * c], 0.0).astype(BF16)))
        for ch in chains:
            ab = ch["aab"].astype(BF16)
            ch["xp"] = _dot(ab, ab)
            ch["t"] = eye + ch["aab"]
        for ch in chains:
            ch["akv"] = _dot(ch["akm"], ch["vv"])
        for stage in range(5):
            for ch in chains:
                xb = ch["xp"].astype(BF16)
                if stage < 4:
                    res = _dot(jnp.concatenate([ch["t"].astype(BF16), xb], axis=0), xb)
                    ch["t"] = ch["t"] + res[0:c]
                    ch["xp"] = res[c:2 * c]
                else:
                    ch["t"] = ch["t"] + _dot(ch["t"].astype(BF16), xb)
        for ch in chains:
            z = jnp.concatenate([ch["akv"].astype(BF16), ch["atm"]], axis=1)
            ch["tz"] = _dot(ch["t"].astype(BF16), z)
        for i in range(0, len(chains), 2):
            c0, c1 = chains[i], chains[i + 1]
            ls = slice(c0["p"] * LANES, (c0["p"] + 1) * LANES)
            w1_s[pl.ds(c0["r0"], c), ls] = jnp.where(lo, c0["tz"][:, 0:LANES], c1["tz"][:, 0:LANES])
            w2_s[pl.ds(c0["r0"], c), ls] = (c0["tz"][:, LANES:2 * LANES] + c1["tz"][:, LANES:2 * LANES]).astype(BF16)
        return carry

    lax.fori_loop(0, tm // (2 * c), solve, 0)

    def scan(ci_, carry):
        r0 = pl.multiple_of(ci_ * c, c)
        pts, rps, bks, vbs = [], [], [], []
        for p in range(npair):
            ls = slice(p * LANES, (p + 1) * LANES)
            pt = st_ref[p]
            rw = jnp.concatenate([rt_s[pl.ds(r0, c), ls], w2_s[pl.ds(r0, c), ls]], axis=0)
            pts.append(pt)
            rps.append(_dot_nt(rw, pt.astype(BF16)))
        uvs = []
        for p in range(npair):
            ls = slice(p * LANES, (p + 1) * LANES)
            u = w1_s[pl.ds(r0, c), ls] + rps[p][c:2 * c]
            vb = vb_s[pl.ds(r0, c), ls]
            uvs.append(jnp.concatenate([u.astype(BF16), vb], axis=0))
            bks.append(jnp.concatenate([bt_s[pl.ds(r0, c), ls], kt_s[pl.ds(r0, c), ls]], axis=0))
        for p in range(npair):
            ls = slice(p * LANES, (p + 1) * LANES)
            upd = _dot_tn(uvs[p], bks[p])
            eg_last = eg_s[pl.ds(pl.multiple_of(r0 + c - 8, 8), 8), ls][7:8, :]
            st_ref[p] = jnp.where(bdmask, (pts[p] + upd) * eg_last, 0.0)
        for p in range(npair):
            ls = slice(p * LANES, (p + 1) * LANES)
            y0 = _dot(tops_s[pl.ds(r0, c), (2 * p) * LANES:(2 * p + 1) * LANES], uvs[p])
            y1 = _dot(tops_s[pl.ds(r0, c), (2 * p + 1) * LANES:(2 * p + 2) * LANES], uvs[p])
            y_s[pl.ds(r0, c), ls] = rps[p][0:c] + jnp.where(lo, y0, y1)
        return carry

    lax.fori_loop(0, tm // c, scan, 0)

    y = y_s[...]
    inv = 1.0 / RWKV_HEAD
    mean = head_sum(y, 2) * inv
    yc = y - mean
    var = head_sum(yc * yc, 2) * inv
    yn = yc * lax.rsqrt(var + RWKV_GN_EPS) * lng_ref[...] + lnb_ref[...]
    bonus = head_sum(r * k * rk_ref[...], 2) * v
    o_ref[0] = (yn + bonus) * g


def _rwkv_call(proj, mu, w0, w2, a0, a2, g2, k_k, k_a, r_k, ln_g, ln_b, tm=256):
    bsz, t_len, _ = proj.shape
    w = BRANCH_W
    w2p = jnp.zeros((LANES, w), F32).at[:64].set(w2).astype(BF16)
    a2p = jnp.zeros((LANES, w), F32).at[64:].set(a2).astype(BF16)
    vec = lambda n: pl.BlockSpec((1, n), lambda b, t: (0, 0))
    mat = lambda n: pl.BlockSpec((n, w), lambda b, t: (0, 0))
    return pl.pallas_call(
        functools.partial(_rwkv_kernel, tm=tm),
        grid=(bsz, t_len // tm),
        in_specs=[pl.BlockSpec((1, tm, RWKV_COLS), lambda b, t: (b, t, 0)),
                  vec(RWKV_COLS), vec(w), mat(LANES), vec(w), mat(LANES), mat(LANES),
                  vec(w), vec(w), vec(w), vec(w), vec(w)],
        out_specs=pl.BlockSpec((1, tm, w), lambda b, t: (b, t, 0)),
        out_shape=jax.ShapeDtypeStruct((bsz, t_len, w), F32),
        scratch_shapes=[pltpu.VMEM((8, RWKV_COLS), F32), pltpu.VMEM((w // LANES, LANES, LANES), F32)]
        + [pltpu.VMEM((tm, w), BF16) for _ in range(6)] + [pltpu.VMEM((tm, w), F32) for _ in range(3)]
        + [pltpu.VMEM((tm, 2 * w), BF16)],
        compiler_params=_cparams(("parallel", "arbitrary")),
        name="rwkv7",
    )(proj, mu.reshape(1, -1), w0.reshape(1, w), w2p, a0.reshape(1, w), a2p, g2.astype(BF16),
      k_k.reshape(1, w), k_a.reshape(1, w), r_k.reshape(1, w), ln_g.reshape(1, w), ln_b.reshape(1, w))


def _foxf_kernel(fl_ref, fb_ref, fcol_ref, frow_ref, carry_ref, *, tm):
    @pl.when(pl.program_id(1) == 0)
    def _():
        carry_ref[...] = jnp.zeros_like(carry_ref)

    lf = _log_sigmoid(fl_ref[0] + fb_ref[...])
    cum = _dot_split_rhs(_tri(tm).astype(BF16), lf, 3) + carry_ref[0:1, :]
    carry_ref[0:1, :] = cum[tm - 1:tm, :]
    fcol_ref[0] = cum
    frow_ref[0] = cum.T[0:8, :]


def _foxf_call(proj, f_bias, tm=512):
    bsz, t_len, _ = proj.shape
    fb = jnp.zeros((1, LANES), F32).at[0, :FOX_HEADS].set(f_bias)
    return pl.pallas_call(
        functools.partial(_foxf_kernel, tm=tm),
        grid=(bsz, t_len // tm),
        in_specs=[pl.BlockSpec((1, tm, LANES), lambda b, t: (b, t, 12)),
                  pl.BlockSpec((1, LANES), lambda b, t: (0, 0))],
        out_specs=[pl.BlockSpec((1, tm, LANES), lambda b, t: (b, t, 0)),
                   pl.BlockSpec((1, 8, tm), lambda b, t: (b, 0, t))],
        out_shape=[jax.ShapeDtypeStruct((bsz, t_len, LANES), F32),
                   jax.ShapeDtypeStruct((bsz, 8, t_len), F32)],
        scratch_shapes=[pltpu.VMEM((8, LANES), F32)],
        compiler_params=_cparams(("parallel", "arbitrary")),
        name="fox_cumf",
    )(proj, fb)


LOG2E = 1.4426950408889634


def _fox_kernel(qi_ref, ki_ref, q_ref, k_ref, v_ref, fq_ref, fk_ref, o_ref, m_s, l_s, acc_s, qm_s, *, tq):
    s_idx = pl.program_id(2)
    pair = pl.program_id(1)
    qi = qi_ref[s_idx]
    ki = ki_ref[s_idx]
    lane = lax.broadcasted_iota(jnp.int32, (1, LANES), 1)

    @pl.when(ki == 0)
    def _():
        m_s[...] = jnp.full_like(m_s, -jnp.inf)
        l_s[...] = jnp.zeros_like(l_s)
        acc_s[...] = jnp.zeros_like(acc_s)
        q = q_ref[0] * (FOX_DH ** -0.5 * LOG2E)
        qm_s[0] = jnp.where(lane < FOX_DH, q, 0.0).astype(BF16)
        qm_s[1] = jnp.where(lane < FOX_DH, 0.0, q).astype(BF16)

    kb = k_ref[0].astype(BF16)
    vb = v_ref[0].astype(BF16)
    biases = []
    for h in range(2):
        hh = 2 * pair + h
        f_k = jnp.sum(jnp.where(lane == hh, fk_ref[0], 0.0), axis=-1, keepdims=True)
        rsel = lax.broadcasted_iota(jnp.int32, (8, 1), 0) == hh
        f_0 = jnp.sum(jnp.where(rsel, fq_ref[0][:, 0:LANES], 0.0), axis=0, keepdims=True)[:, 0:1]
        biases.append((f_k - f_0) * LOG2E)

    def step(masked):
        sts = [_dot_nt(kb, qm_s[h]) for h in range(2)]
        ps, alphas = [], []
        for h in range(2):
            s = sts[h] - biases[h]
            if masked:
                krow = lax.broadcasted_iota(jnp.int32, (tq, tq), 0)
                qcol = lax.broadcasted_iota(jnp.int32, (tq, tq), 1)
                s = jnp.where(krow <= qcol, s, -1e30)
            m_old = m_s[h]
            m_new = jnp.maximum(m_old, jnp.max(s, axis=0, keepdims=True))
            alpha = jnp.exp2(m_old - m_new)
            p = jnp.exp2(s - m_new)
            l_s[h] = alpha * l_s[h] + jnp.sum(p, axis=0, keepdims=True)
            m_s[h] = m_new
            ps.append(p.astype(BF16))
            alphas.append(alpha)
        for h in range(2):
            acc_s[h] = alphas[h] * acc_s[h] + _dot_tn(vb, ps[h])

    @pl.when(ki < qi)
    def _():
        step(False)

    @pl.when(ki == qi)
    def _():
        step(True)
        o0 = acc_s[0] / l_s[0]
        o1 = acc_s[1] / l_s[1]
        chan = lax.broadcasted_iota(jnp.int32, (LANES, 1), 0)
        o_ref[0] = jnp.where(chan < FOX_DH, o0, o1).T


def _fox_call(proj, fcol, frow, tq=512):
    bsz, t_len, _ = proj.shape
    nq = t_len // tq
    pairs = [(i, j) for i in range(nq) for j in range(i + 1)]
    qi_tab = jnp.asarray([p[0] for p in pairs], jnp.int32)
    ki_tab = jnp.asarray([p[1] for p in pairs], jnp.int32)
    npairs = FOX_HEADS // 2
    grid_spec = pltpu.PrefetchScalarGridSpec(
        num_scalar_prefetch=2,
        grid=(bsz, npairs, len(pairs)),
        in_specs=[
            pl.BlockSpec((1, tq, LANES), lambda b, p, s, qi, ki: (b, qi[s], p)),
            pl.BlockSpec((1, tq, LANES), lambda b, p, s, qi, ki: (b, ki[s], 4 + p)),
            pl.BlockSpec((1, tq, LANES), lambda b, p, s, qi, ki: (b, ki[s], 8 + p)),
            pl.BlockSpec((1, 8, tq), lambda b, p, s, qi, ki: (b, 0, qi[s])),
            pl.BlockSpec((1, tq, LANES), lambda b, p, s, qi, ki: (b, ki[s], 0)),
        ],
        out_specs=pl.BlockSpec((1, tq, LANES), lambda b, p, s, qi, ki: (b, qi[s], p)),
        scratch_shapes=[pltpu.VMEM((2, 1, tq), F32), pltpu.VMEM((2, 1, tq), F32), pltpu.VMEM((2, LANES, tq), F32),
                        pltpu.VMEM((2, tq, LANES), BF16)],
    )
    return pl.pallas_call(
        functools.partial(_fox_kernel, tq=tq),
        grid_spec=grid_spec,
        out_shape=jax.ShapeDtypeStruct((bsz, t_len, BRANCH_W), F32),
        compiler_params=_cparams(("parallel", "parallel", "arbitrary")),
        name="fox_attn",
    )(qi_tab, ki_tab, proj, proj, proj, frow, fcol)


def _merge_kernel(x_ref, h_ref, ba_ref, bb_ref, bc_ref, bd_ref, mod_ref, wg_ref, bg_ref, wbr_ref, wo_ref,
                  l1g_ref, l1b_ref, rw_ref, rb_ref, x1_ref, h2_ref, cmb_ref):
    hb = h_ref[0]
    merged = None
    for n, br_ref in enumerate((ba_ref, bb_ref, bc_ref, bd_ref)):
        gate = _sigmoid(_dot(hb, wg_ref[n]) + bg_ref[n:n + 1, :])
        term = gate * _dot(br_ref[0].astype(BF16), wbr_ref[n])
        merged = term if merged is None else merged + term
    y = _dot(merged.astype(BF16), wo_ref[...])
    gt1 = mod_ref[0, 2:3, :]
    x1 = _ln(DN_ALPHA * x_ref[0] + gt1 * y) * l1g_ref[...] + l1b_ref[...]
    x1_ref[0] = x1
    h2 = _ln(x1) * (1.0 + mod_ref[0, 4:5, :]) + mod_ref[0, 3:4, :]
    h2_ref[0] = h2.astype(BF16)

    tm = x1.shape[0]
    lane = lax.broadcasted_iota(jnp.int32, (tm, LANES), 1)
    lanef = lane.astype(F32)
    neg = -jnp.inf
    h_hi, h_lo = _split_bf16(h2, 2)
    r_hi, r_lo = _split_bf16(rw_ref[...], 2)
    logits = jnp.where(lane < N_EXPERTS, _dot(h_hi, r_hi) + (_dot(h_hi, r_lo) + _dot(h_lo, r_hi)), neg)
    ex = jnp.exp(logits - jnp.max(logits, axis=-1, keepdims=True))
    probs = ex / jnp.sum(ex, axis=-1, keepdims=True)
    sel = probs + rb_ref[...]
    grp = lane // 4

    def top2(vals):
        m1 = jnp.max(vals, axis=-1, keepdims=True)
        i1 = jnp.min(jnp.where(vals == m1, lanef, 999.0), axis=-1, keepdims=True)
        rest = jnp.where(lanef == i1, neg, vals)
        m2 = jnp.max(rest, axis=-1, keepdims=True)
        i2 = jnp.min(jnp.where(rest == m2, lanef, 999.0), axis=-1, keepdims=True)
        return m1, i1, m2, i2

    best = None
    for gidx in range(4):
        m1, _, m2, _ = top2(jnp.where(grp == gidx, sel, neg))
        score = m1 + m2
        if best is None:
            best, gsel = score, jnp.zeros_like(score, dtype=jnp.int32)
        else:
            better = score > best
            gsel = jnp.where(better, gidx, gsel)
            best = jnp.where(better, score, best)
    _, i1, _, i2 = top2(jnp.where(grp == gsel, sel, neg))
    w1 = jnp.sum(jnp.where(lanef == i1, probs, 0.0), axis=-1, keepdims=True)
    w2 = jnp.sum(jnp.where(lanef == i2, probs, 0.0), axis=-1, keepdims=True)
    cmb_ref[0] = (jnp.where(lanef == i1, w1, 0.0) + jnp.where(lanef == i2, w2, 0.0)) / (w1 + w2)


def _merge_call(x, h, branches, mod, wg, bg, wbr, wo, l1g, l1b, rw, rb, tm=512):
    bsz, t_len, _ = x.shape
    tok = lambda w: pl.BlockSpec((1, tm, w), lambda b, t: (b, t, 0))
    const = lambda shape: pl.BlockSpec(shape, lambda b, t: (0,) * len(shape), pipeline_mode=pl.Buffered(1))
    return pl.pallas_call(
        _merge_kernel,
        grid=(bsz, t_len // tm),
        in_specs=[tok(D_MODEL), tok(D_MODEL)] + [tok(BRANCH_W)] * 4
        + [pl.BlockSpec((1, N_ADA, D_MODEL), lambda b, t: (b, 0, 0)),
           const((4, D_MODEL, D_MODEL)), const((4, D_MODEL)), const((4, BRANCH_W, D_MODEL)), const((D_MODEL, D_MODEL)),
           const((1, D_MODEL)), const((1, D_MODEL)), const((D_MODEL, LANES)), const((1, LANES))],
        out_specs=[tok(D_MODEL), tok(D_MODEL), tok(LANES)],
        out_shape=[jax.ShapeDtypeStruct((bsz, t_len, D_MODEL), F32),
                   jax.ShapeDtypeStruct((bsz, t_len, D_MODEL), BF16),
                   jax.ShapeDtypeStruct((bsz, t_len, LANES), F32)],
        compiler_params=_cparams(("parallel", "parallel")),
        name="merge",
    )(x, h, *branches, mod, wg, bg, wbr, wo, l1g, l1b, rw, rb)


def _moe_kernel(h_ref, cmb_ref, x_ref, mod_ref, wg_ref, wu_ref, wd_ref, l2g_ref, l2b_ref, o_ref, acc_ref):
    e = pl.program_id(2)

    @pl.when(e == 0)
    def _():
        acc_ref[...] = jnp.zeros_like(acc_ref)

    hb = h_ref[0]
    a = _dot(hb, wg_ref[0])
    u = _dot(hb, wu_ref[0])
    he = (a * _sigmoid(a) * u).astype(BF16)
    cmb = cmb_ref[0]
    lane = lax.broadcasted_iota(jnp.int32, cmb.shape, 1)
    ce = jnp.sum(jnp.where(lane == e, cmb, 0.0), axis=-1, keepdims=True)
    acc_ref[...] += ce * _dot(he, wd_ref[0])

    @pl.when(e == N_EXPERTS - 1)
    def _():
        gt2 = mod_ref[0, 5:6, :]
        o_ref[0] = _ln(DN_ALPHA * x_ref[0] + gt2 * acc_ref[...]) * l2g_ref[...] + l2b_ref[...]


def _moe_call(h2, cmb, x1, mod, wg, wu, wd, l2g, l2b, tm=1024):
    bsz, t_len, _ = x1.shape
    tok = lambda w: pl.BlockSpec((1, tm, w), lambda b, t, e: (b, t, 0))
    return pl.pallas_call(
        _moe_kernel,
        grid=(bsz, t_len // tm, N_EXPERTS),
        in_specs=[tok(D_MODEL), tok(LANES), tok(D_MODEL),
                  pl.BlockSpec((1, N_ADA, D_MODEL), lambda b, t, e: (b, 0, 0)),
                  pl.BlockSpec((1, D_MODEL, D_FF_EXPERT), lambda b, t, e: (e, 0, 0)),
                  pl.BlockSpec((1, D_MODEL, D_FF_EXPERT), lambda b, t, e: (e, 0, 0)),
                  pl.BlockSpec((1, D_FF_EXPERT, D_MODEL), lambda b, t, e: (e, 0, 0)),
                  pl.BlockSpec((1, D_MODEL), lambda b, t, e: (0, 0)),
                  pl.BlockSpec((1, D_MODEL), lambda b, t, e: (0, 0))],
        out_specs=tok(D_MODEL),
        out_shape=jax.ShapeDtypeStruct((bsz, t_len, D_MODEL), F32),
        scratch_shapes=[pltpu.VMEM((tm, D_MODEL), F32)],
        compiler_params=_cparams(("parallel", "parallel", "arbitrary")),
        name="moe",
    )(h2, cmb, x1, mod, wg, wu, wd, l2g, l2b)


def _pad_cols(w, n):
    return jnp.pad(w, ((0, 0), (0, n - w.shape[1])))


def kernel(x, c, ada_w, ada_b, w_in, gla_alpha_up, gla_alpha_b, gla_norm_g, rwkv_mu, rwkv_w0, rwkv_w2, rwkv_a0, rwkv_a2, rwkv_g2, rwkv_k_k, rwkv_k_a, rwkv_r_k, rwkv_ln_g, rwkv_ln_b, fox_f_bias, hgrn_lb_logits, hgrn_norm_g, w_br, w_gate, b_gate, w_o, ln1_g, ln1_b, router_w, router_b, exp_w_gate, exp_w_up, exp_w_down, ln2_g, ln2_b):
    mod_all = _ada_call(c, ada_w, ada_b)
    rw = _pad_cols(router_w, LANES)
    rb = _pad_cols(router_b.reshape(1, N_EXPERTS), LANES)
    o1, o2, o3 = GLA_COLS, GLA_COLS + RWKV_COLS, GLA_COLS + RWKV_COLS + FOX_COLS
    for i in range(DEPTH):
        mod = mod_all[i]
        wi = w_in[i]
        wa = _pad_cols(wi[:, :o1], PAD_COLS).astype(BF16)
        wb = wi[:, o1:o2].astype(BF16)
        wc = _pad_cols(wi[:, o2:o3], PAD_COLS).astype(BF16)
        wd = wi[:, o3:].astype(BF16)
        h, pa, pb, pc, pd = _inproj_call(x, mod, wa, wb, wc, wd)
        br_a = _gla_call(pa, gla_alpha_up[i], gla_alpha_b[i], gla_norm_g[i])
        br_b = _rwkv_call(pb, rwkv_mu[i], rwkv_w0[i], rwkv_w2[i], rwkv_a0[i], rwkv_a2[i], rwkv_g2[i],
                          rwkv_k_k[i], rwkv_k_a[i], rwkv_r_k[i], rwkv_ln_g[i], rwkv_ln_b[i])
        fcol, frow = _foxf_call(pc, fox_f_bias[i])
        br_c = _fox_call(pc, fcol, frow)
        br_d = _hgrn_call(pd, hgrn_lb_logits, hgrn_norm_g[i], i)
        x1, h2, cmb = _merge_call(
            x, h, (br_a, br_b, br_c, br_d), mod, w_gate[i].astype(BF16), b_gate[i], w_br[i].astype(BF16),
            w_o[i].astype(BF16), ln1_g[i].reshape(1, -1), ln1_b[i].reshape(1, -1), rw, rb)
        x = _moe_call(h2, cmb, x1, mod, exp_w_gate[i].astype(BF16), exp_w_up[i].astype(BF16),
                      exp_w_down[i].astype(BF16), ln2_g[i].reshape(1, -1), ln2_b[i].reshape(1, -1))
    return x
```

```python
import functools

import jax
import jax.numpy as jnp
from jax import lax
from jax.experimental import pallas as pl
from jax.experimental.pallas import tpu as pltpu

F32 = jnp.float32
BF16 = jnp.bfloat16
HI = lax.Precision.HIGHEST

D_MODEL = 1024
DEPTH = 2
N_ADA = 6
BRANCH_W = 512
GLA_HEADS = 4
GLA_DK = 64
GLA_KW = 256
GLA_GATE_RANK = 16
GLA_LOGIT_NORM = 16.0
RWKV_HEAD = 64
RWKV_COLS = 1792
RWKV_GN_EPS = 64e-5
FOX_HEADS = 8
FOX_DH = 64
HGRN_KW = 512
N_EXPERTS = 16
D_FF_EXPERT = 512
DN_ALPHA = (2.0 * DEPTH) ** 0.25
GLA_COLS = 1552
FOX_COLS = 1544
HGRN_COLS = 2048
PAD_COLS = 1664

LANES = 128
CHUNK = 64
SUB = 16
FACTORED_MAX_DECAY = 80.0
FACTORED_MAX_ABS = 1e3
VMEM_LIMIT = 56 * 1024 * 1024


def _cparams(sem):
    return pltpu.CompilerParams(dimension_semantics=sem, vmem_limit_bytes=VMEM_LIMIT)


def _sigmoid(x):
    return 1.0 / (1.0 + jnp.exp(-x))


def _log_sigmoid(x):
    return jnp.minimum(x, 0.0) - jnp.log(1.0 + jnp.exp(-jnp.abs(x)))


def _ln(x, eps=1e-5):
    mu = jnp.mean(x, axis=-1, keepdims=True)
    xc = x - mu
    var = jnp.mean(xc * xc, axis=-1, keepdims=True)
    return xc * lax.rsqrt(var + eps)


def _tri(n, strict=False):
    r = lax.broadcasted_iota(jnp.int32, (n, n), 0)
    c = lax.broadcasted_iota(jnp.int32, (n, n), 1)
    return jnp.where((c < r) if strict else (c <= r), 1.0, 0.0).astype(F32)


def _dot(a, b, prec=None):
    return jnp.dot(a, b, preferred_element_type=F32, precision=prec)


def _dot_nt(a, b, prec=None):
    return lax.dot_general(a, b, (((1,), (1,)), ((), ())), preferred_element_type=F32, precision=prec)


def _dot_tn(a, b, prec=None):
    return lax.dot_general(a, b, (((0,), (0,)), ((), ())), preferred_element_type=F32, precision=prec)


def _ada_kernel(c_ref, w_ref, b_ref, o_ref):
    c = c_ref[...]
    cond = c * _sigmoid(c)
    o_ref[0] = _dot(cond, w_ref[0], HI) + b_ref[0]


def _ada_call(c, ada_w, ada_b):
    bsz = c.shape[0]
    n = N_ADA * D_MODEL
    bn = 1536
    out = pl.pallas_call(
        _ada_kernel,
        grid=(DEPTH, n // bn),
        in_specs=[
            pl.BlockSpec((bsz, D_MODEL), lambda i, j: (0, 0)),
            pl.BlockSpec((1, D_MODEL, bn), lambda i, j: (i, 0, j)),
            pl.BlockSpec((1, 1, bn), lambda i, j: (i, 0, j)),
        ],
        out_specs=pl.BlockSpec((1, bsz, bn), lambda i, j: (i, 0, j)),
        out_shape=jax.ShapeDtypeStruct((DEPTH, bsz, n), F32),
        compiler_params=_cparams(("parallel", "parallel")),
        name="ada_mod",
    )(c, ada_w, ada_b.reshape(DEPTH, 1, n))
    return out.reshape(DEPTH, bsz, N_ADA, D_MODEL)


def _inproj_kernel(x_ref, mod_ref, wa_ref, wb_ref, wc_ref, wd_ref,
                   h_ref, oa_ref, ob_ref, ocq_ref, ockv_ref, ocf_ref, od_ref):
    x = x_ref[0]
    sh = mod_ref[0, 0:1, :]
    sc = mod_ref[0, 1:2, :]
    hb = (_ln(x) * (1.0 + sc) + sh).astype(BF16)
    h_ref[0] = hb
    oa_ref[0] = _dot(hb, wa_ref[...])
    ob_ref[0] = _dot(hb, wb_ref[...])
    oc = _dot(hb, wc_ref[...])
    w = BRANCH_W
    ocq_ref[0] = oc[:, 0:w]
    ockv_ref[0] = oc[:, w:3 * w].astype(BF16)
    ocf_ref[0] = oc[:, 3 * w:3 * w + LANES]
    od_ref[0] = _dot(hb, wd_ref[...])


def _inproj_call(x, mod, wa, wb, wc, wd, tm=256):
    bsz, t_len, _ = x.shape
    widths = (wa.shape[1], wb.shape[1], wc.shape[1], wd.shape[1])
    outs = [(D_MODEL, BF16), (widths[0], F32), (widths[1], F32), (BRANCH_W, F32), (2 * BRANCH_W, BF16),
            (LANES, F32), (widths[3], F32)]

    def wspec(w):
        return pl.BlockSpec((D_MODEL, w), lambda b, t: (0, 0), pipeline_mode=pl.Buffered(1))

    return pl.pallas_call(
        _inproj_kernel,
        grid=(bsz, t_len // tm),
        in_specs=[pl.BlockSpec((1, tm, D_MODEL), lambda b, t: (b, t, 0)),
                  pl.BlockSpec((1, N_ADA, D_MODEL), lambda b, t: (b, 0, 0))] + [wspec(w) for w in widths],
        out_specs=[pl.BlockSpec((1, tm, w), lambda b, t: (b, t, 0)) for w, _ in outs],
        out_shape=[jax.ShapeDtypeStruct((bsz, t_len, w), dt) for w, dt in outs],
        compiler_params=_cparams(("parallel", "parallel")),
        name="inproj",
    )(x, mod, wa, wb, wc, wd)


def _gla_kernel(*refs, mode, layer, hp, qscale, tm, ngrp):
    if mode == "gla":
        q_ref, k_ref, v_ref, gate_ref, al_ref, aup_ref, ab_ref, ng_ref, sel_ref, o_ref = refs[:10]
    else:
        q_ref, k_ref, v_ref, gate_ref, lbl_ref, ng_ref, sel_ref, o_ref = refs[:8]
    st_ref, q_s, k_s, b_s, qe_s, kd_s, pw_s = refs[-7:]
    c = CHUNK
    nb = c // SUB
    dk = LANES // hp

    @pl.when(pl.program_id(1) == 0)
    def _():
        st_ref[...] = jnp.zeros_like(st_ref)

    q = q_ref[0]
    if qscale != 1.0:
        q = q * qscale
    kin = k_ref[0]
    if mode == "gla":
        g = _log_sigmoid(_dot(al_ref[0], aup_ref[...], HI) + ab_ref[...]) / GLA_LOGIT_NORM
        k = kin
    else:
        lg = lbl_ref[...]
        mx = jnp.max(lg, axis=0, keepdims=True)
        ex = jnp.exp(lg - mx)
        p = ex / jnp.sum(ex, axis=0, keepdims=True)
        cum = p[0:1, :]
        for j in range(1, layer + 1):
            cum = cum + p[j:j + 1, :]
        lb = cum - p[0:1, :]
        g = jnp.log(lb + (1.0 - lb) * _sigmoid(kin))
        k = (1.0 - lb) * _sigmoid(-kin)
    tr = lax.broadcasted_iota(jnp.int32, (tm, tm), 0)
    tc = lax.broadcasted_iota(jnp.int32, (tm, tm), 1)
    same = (tr // c) == (tc // c)
    b = _dot_split_rhs(jnp.where((tc <= tr) & same, 1.0, 0.0).astype(BF16), g, 3)
    nch = tm // c
    b_end = jnp.concatenate([jnp.broadcast_to(b[(ci + 1) * c - 1:(ci + 1) * c, :], (c, b.shape[1]))
                             for ci in range(nch)], axis=0)
    qe = (q * jnp.exp(b)).astype(BF16)
    kd = (k * jnp.exp(b_end - b)).astype(BF16)

    lane = lax.broadcasted_iota(jnp.int32, (1, LANES), 1)
    hmask = [((lane >= h * dk) & (lane < (h + 1) * dk)) for h in range(hp)]
    row = lax.broadcasted_iota(jnp.int32, (c, c), 0)
    col = lax.broadcasted_iota(jnp.int32, (c, c), 1)
    off_mask = (col // SUB) < (row // SUB)
    diag_mask = ((col // SUB) == (row // SUB)) & (col <= row)
    ng = ng_ref[...]

    def head_lanes(x, h):
        return x if hp == 1 else jnp.where(hmask[h], x, jnp.zeros_like(x))

    def finish(rows, hv, o):
        o = o * lax.rsqrt(jnp.mean(o * o, axis=-1, keepdims=True) + 1e-6) * ng
        gt = gate_ref[0, rows, hv * LANES:(hv + 1) * LANES]
        o_ref[0, rows, hv * LANES:(hv + 1) * LANES] = o * (gt * _sigmoid(gt))

    def intra_exact(rows):
        qhs, khats = {}, []
        for gi in range(ngrp):
            gl = slice(gi * LANES, (gi + 1) * LANES)
            qc = q_s[rows, gl]
            kc = k_s[rows, gl]
            bc = b_s[rows, gl]
            rblk = jnp.concatenate([jnp.broadcast_to(bc[m * SUB:m * SUB + 1, :], (SUB, LANES)) for m in range(nb)],
                                   axis=0)
            qh = (qc * jnp.exp(bc - rblk)).astype(BF16)
            for h in range(hp):
                qhs[gi * hp + h] = head_lanes(qh, h)
            khats.append([None] + [(kc * jnp.exp(jnp.minimum(bc[m * SUB:m * SUB + 1, :] - bc, 0.0))).astype(BF16)
                                   for m in range(1, nb)])
            for jj in range(SUB):
                kb = jnp.concatenate([jnp.broadcast_to(kc[m * SUB + jj:m * SUB + jj + 1, :], (SUB, LANES))
                                      for m in range(nb)], axis=0)
                bb = jnp.concatenate([jnp.broadcast_to(bc[m * SUB + jj:m * SUB + jj + 1, :], (SUB, LANES))
                                      for m in range(nb)], axis=0)
                pw_s[gi * c:(gi + 1) * c, jj * LANES:(jj + 1) * LANES] = (
                    qc * kb * jnp.exp(jnp.minimum(bc - bb, 0.0))).astype(BF16)
        a_offs = {}
        for hv in range(ngrp * hp):
            a_offs[hv] = jnp.concatenate(
                [jnp.zeros((SUB, c), F32)]
                + [_dot_nt(qhs[hv][m * SUB:(m + 1) * SUB, :], khats[hv // hp][m]) for m in range(1, nb)], axis=0)
        a_diags = [_dot(pw_s[...], sel_ref[h]) for h in range(hp)]
        out = {}
        for hv in range(ngrp * hp):
            gi, h = hv // hp, hv % hp
            out[hv] = jnp.where(off_mask, a_offs[hv], jnp.where(diag_mask, a_diags[h][gi * c:(gi + 1) * c], 0.0))
        return out

    def chunk(ci, carry):
        r0 = pl.multiple_of(ci * c, c)
        rows = pl.ds(r0, c)
        sts, o_int, upds, vbs = [], {}, {}, {}
        for gi in range(ngrp):
            gl = slice(gi * LANES, (gi + 1) * LANES)
            sts.append(st_ref[gi])
            stb = sts[gi].astype(BF16)
            for h in range(hp):
                hv = gi * hp + h
                vbs[hv] = v_ref[0, rows, hv * LANES:(hv + 1) * LANES].astype(BF16)
                o_int[hv] = _dot_nt(head_lanes(qe_s[rows, gl], h), stb)
                upds[hv] = _dot_tn(vbs[hv], kd_s[rows, gl])
        attn = intra_exact(rows)
        for hv in range(ngrp * hp):
            finish(rows, hv, o_int[hv] + _dot(attn[hv].astype(BF16), vbs[hv]))
        for gi in range(ngrp):
            gl = slice(gi * LANES, (gi + 1) * LANES)
            b_last = b_s[pl.ds(pl.multiple_of(r0 + c - 8, 8), 8), gl][7:8, :]
            new_st = sts[gi] * jnp.exp(b_last)
            for h in range(hp):
                new_st = new_st + head_lanes(upds[gi * hp + h], h)
            st_ref[gi] = new_st
        return carry

    small_decay = jnp.max(jnp.maximum(-b * (1.0 / FACTORED_MAX_DECAY),
                                      jnp.maximum(jnp.abs(q), jnp.abs(k)) * (1.0 / FACTORED_MAX_ABS))) < 1.0

    @pl.when(small_decay)
    def _():
        nch = tm // c
        heads = [(ci, hv) for ci in range(nch) for hv in range(ngrp * hp)]
        b0 = jnp.concatenate([jnp.broadcast_to(b[ci * c:ci * c + 1, :], (c, b.shape[1])) for ci in range(nch)], axis=0)
        bz = b - b0
        qz = (q * jnp.exp(bz)).astype(BF16)
        kz = (k * jnp.exp(-bz)).astype(BF16)
        sl = lambda ci, gi: (slice(ci * c, (ci + 1) * c), slice(gi * LANES, (gi + 1) * LANES))
        vbs, upds, attn, o_intra, o_int = {}, {}, {}, {}, {}
        for ci, hv in heads:
            vbs[ci, hv] = v_ref[0, ci * c:(ci + 1) * c, hv * LANES:(hv + 1) * LANES].astype(BF16)
            upds[ci, hv] = _dot_tn(vbs[ci, hv], kd[sl(ci, hv // hp)])
        for ci, hv in heads:
            attn[ci, hv] = jnp.where(col <= row, _dot_nt(head_lanes(qz[sl(ci, hv // hp)], hv % hp),
                                                           kz[sl(ci, hv // hp)]), 0.0).astype(BF16)
        for ci, hv in heads:
            o_intra[ci, hv] = _dot(attn[ci, hv], vbs[ci, hv])
        sts = [[st_ref[gi]] for gi in range(ngrp)]
        for ci in range(nch):
            for gi in range(ngrp):
                new_st = sts[gi][ci] * jnp.exp(b[(ci + 1) * c - 1:(ci + 1) * c, gi * LANES:(gi + 1) * LANES])
                for h in range(hp):
                    new_st = new_st + head_lanes(upds[ci, gi * hp + h], h)
                sts[gi].append(new_st)
        for gi in range(ngrp):
            st_ref[gi] = sts[gi][nch]
        for ci, hv in heads:
            o_int[ci, hv] = _dot_nt(head_lanes(qe[sl(ci, hv // hp)], hv % hp), sts[hv // hp][ci].astype(BF16))
        for ci, hv in heads:
            finish(slice(ci * c, (ci + 1) * c), hv, o_int[ci, hv] + o_intra[ci, hv])

    @pl.when(jnp.logical_not(small_decay))
    def _():
        q_s[...] = q
        k_s[...] = k
        b_s[...] = b
        qe_s[...] = qe
        kd_s[...] = kd
        lax.fori_loop(0, nch, chunk, 0)


def _pair_select(hp):
    dk = LANES // hp
    r = jnp.arange(SUB * LANES)
    jj = r // LANES
    head = (r % LANES) // dk
    cj = jnp.arange(CHUNK) % SUB
    sel = (jj[None, :, None] == cj[None, None, :]) & (head[None, :, None] == jnp.arange(hp)[:, None, None])
    return sel.astype(BF16)


def _gla_scratch(tm, ngrp):
    w = ngrp * LANES
    return ([pltpu.VMEM((ngrp, LANES, LANES), F32)] + [pltpu.VMEM((tm, w), F32) for _ in range(3)]
            + [pltpu.VMEM((tm, w), BF16) for _ in range(2)] + [pltpu.VMEM((ngrp * CHUNK, SUB * LANES), BF16)])


def _gla_call(proj, alpha_up, alpha_b, norm_g, tm=256):
    bsz, t_len, _ = proj.shape
    hp = 2
    ngrp = GLA_HEADS // hp
    aup = jnp.zeros((LANES, GLA_KW), F32).at[:GLA_GATE_RANK].set(alpha_up)
    blk = lambda w, j: pl.BlockSpec((1, tm, w), lambda b, t: (b, t, j))
    const = lambda shape: pl.BlockSpec(shape, lambda b, t: (0,) * len(shape))
    kern = functools.partial(_gla_kernel, mode="gla", layer=0, hp=hp, qscale=GLA_DK ** -0.5, tm=tm, ngrp=ngrp)
    return pl.pallas_call(
        kern,
        grid=(bsz, t_len // tm),
        in_specs=[blk(GLA_KW, 0), blk(GLA_KW, 1), blk(BRANCH_W, 1), blk(BRANCH_W, 2), blk(LANES, 12),
                  const((LANES, GLA_KW)), const((1, GLA_KW)), const((1, LANES)), const((hp, SUB * LANES, CHUNK))],
        out_specs=blk(BRANCH_W, 0),
        out_shape=jax.ShapeDtypeStruct((bsz, t_len, BRANCH_W), F32),
        scratch_shapes=_gla_scratch(tm, ngrp),
        compiler_params=_cparams(("parallel", "arbitrary")),
        name="gla",
    )(proj, proj, proj, proj, proj, aup, alpha_b.reshape(1, GLA_KW), norm_g.reshape(1, LANES), _pair_select(hp))


def _hgrn_call(proj, lb_logits, norm_g, layer, tm=256):
    bsz, t_len, _ = proj.shape
    ngrp = HGRN_KW // LANES
    blk = lambda j: pl.BlockSpec((1, tm, HGRN_KW), lambda b, t: (b, t, j))
    const = lambda shape: pl.BlockSpec(shape, lambda b, t: (0,) * len(shape))
    kern = functools.partial(_gla_kernel, mode="hgrn", layer=layer, hp=1, qscale=1.0, tm=tm, ngrp=ngrp)
    return pl.pallas_call(
        kern,
        grid=(bsz, t_len // tm),
        in_specs=[blk(0), blk(1), blk(2), blk(3), const((DEPTH, HGRN_KW)), const((1, LANES)),
                  const((1, SUB * LANES, CHUNK))],
        out_specs=blk(0),
        out_shape=jax.ShapeDtypeStruct((bsz, t_len, BRANCH_W), F32),
        scratch_shapes=_gla_scratch(tm, ngrp),
        compiler_params=_cparams(("parallel", "arbitrary")),
        name="hgrn",
    )(proj, proj, proj, proj, lb_logits, norm_g.reshape(1, LANES), _pair_select(1))


def _split_bf16(x, n):
    parts, rest = [], x
    for i in range(n):
        part = rest.astype(BF16)
        parts.append(part)
        if i + 1 < n:
            rest = rest - part.astype(F32)
    return parts


def _dot_split_rhs(a01, x, n):
    out = None
    for part in _split_bf16(x, n):
        term = _dot(a01, part)
        out = term if out is None else out + term
    return out


def _dot_split_lhs(x, b01, n):
    out = None
    for part in _split_bf16(x, n):
        term = _dot(part, b01)
        out = term if out is None else out + term
    return out


def _rwkv_kernel(cols_ref, mu_ref, w0_ref, w2_ref, a0_ref, a2_ref, g2_ref, kk_ref, ka_ref, rk_ref, lng_ref, lnb_ref,
                 o_ref, carry_ref, st_ref, rt_s, at_s, bt_s, kt_s, vb_s, w2_s, eg_s, y_s, w1_s, tops_s, *, tm):
    w = BRANCH_W
    c = CHUNK
    npair = w // LANES

    @pl.when(pl.program_id(1) == 0)
    def _():
        carry_ref[...] = jnp.zeros_like(carry_ref)
        st_ref[...] = jnp.zeros_like(st_ref)

    cols = cols_ref[0]
    rowi = lax.broadcasted_iota(jnp.int32, cols.shape, 0)
    prev = jnp.where(rowi == 0, carry_ref[0:1, :], pltpu.roll(cols, 1, 0))
    carry_ref[0:1, :] = cols[tm - 1:tm, :]
    xs = cols + (prev - cols) * mu_ref[...]
    r = xs[:, 0:w]
    k = xs[:, w:2 * w]
    v = xs[:, 2 * w:3 * w]
    wl = xs[:, 3 * w:3 * w + LANES]
    gl = xs[:, 3 * w + LANES:3 * w + 2 * LANES]
    w_raw = _log_sigmoid(w0_ref[...] + _dot(jnp.tanh(wl).astype(BF16), w2_ref[...])) - 0.5
    lw = -jnp.exp(w_raw)
    a = _sigmoid(a0_ref[...] + _dot(wl.astype(BF16), a2_ref[...]))
    g = _dot(_sigmoid(gl).astype(BF16), g2_ref[...])

    hr = lax.broadcasted_iota(jnp.int32, (LANES, LANES), 0) // RWKV_HEAD
    hc = lax.broadcasted_iota(jnp.int32, (LANES, LANES), 1) // RWKV_HEAD
    bdmask = hr == hc
    bd = jnp.where(bdmask, 1.0, 0.0).astype(BF16)

    def head_sum(t, n):
        return jnp.concatenate([_dot_split_lhs(t[:, p * LANES:(p + 1) * LANES], bd, n) for p in range(npair)], axis=1)

    kk = k * kk_ref[...]
    kk = kk / jnp.maximum(jnp.sqrt(head_sum(kk * kk, 3)), 1e-12)
    k = k * (1.0 + (a - 1.0) * ka_ref[...])

    tr = lax.broadcasted_iota(jnp.int32, (tm, tm), 0)
    tc = lax.broadcasted_iota(jnp.int32, (tm, tm), 1)
    tri = jnp.where((tc <= tr) & ((tr // c) == (tc // c)), 1.0, 0.0).astype(BF16)
    gam = _dot_split_rhs(tri, lw, 3)
    eg = jnp.exp(gam)
    ieg = jnp.exp(-gam)
    rt_s[...] = (r * eg).astype(BF16)
    at_s[...] = (-kk * jnp.exp(gam - lw)).astype(BF16)
    bt_s[...] = (kk * a * ieg).astype(BF16)
    kt_s[...] = (k * ieg).astype(BF16)
    vb_s[...] = v.astype(BF16)
    eg_s[...] = eg

    lane = lax.broadcasted_iota(jnp.int32, (1, LANES), 1)
    lo = lane < RWKV_HEAD
    hmask = [lo, jnp.logical_not(lo)]
    row = lax.broadcasted_iota(jnp.int32, (c, LANES), 0)
    col = lax.broadcasted_iota(jnp.int32, (c, LANES), 1)
    top_mask = (col % c) <= row
    ak_mask = (col >= c) & ((col - c) < row)
    r64 = lax.broadcasted_iota(jnp.int32, (c, c), 0)
    c64 = lax.broadcasted_iota(jnp.int32, (c, c), 1)
    strict = c64 < r64
    eye = jnp.where(c64 == r64, 1.0, 0.0).astype(F32)

    def solve(j, carry):
        base = j * (2 * c)
        chains = []
        for cc in range(2):
            r0 = pl.multiple_of(base + cc * c, c)
            for p in range(npair):
                ls = slice(p * LANES, (p + 1) * LANES)
                at = at_s[pl.ds(r0, c), ls]
                vb = vb_s[pl.ds(r0, c), ls]
                ra = jnp.concatenate([rt_s[pl.ds(r0, c), ls], at], axis=0)
                bk = jnp.concatenate([bt_s[pl.ds(r0, c), ls], kt_s[pl.ds(r0, c), ls]], axis=0)
                vv = jnp.concatenate([vb, vb], axis=0)
                for h in range(2):
                    x = _dot_nt(jnp.where(hmask[h], ra, jnp.zeros_like(ra)), bk)
                    tops_s[pl.ds(r0, c), (2 * p + h) * LANES:(2 * p + h + 1) * LANES] = (
                        jnp.where(top_mask, x[0:c], 0.0).astype(BF16))
                    chains.append(dict(r0=r0, p=p, h=h, vv=vv, atm=jnp.where(hmask[h], at, jnp.zeros_like(at)),
                                       aab=jnp.where(strict, x[c:2 * c, 0:c], 0.0),
                                       akm=jnp.where(ak_mask, x[c:2 * c], 0.0).astype(BF16)))
        for ch in chains:
            ab = ch["aab"].astype(BF16)
            ch["xp"] = _dot(ab, ab)
            ch["t"] = eye + ch["aab"]
        for ch in chains:
            ch["akv"] = _dot(ch["akm"], ch["vv"])
        for stage in range(5):
            for ch in chains:
                xb = ch["xp"].astype(BF16)
                if stage < 4:
                    res = _dot(jnp.concatenate([ch["t"].astype(BF16), xb], axis=0), xb)
                    ch["t"] = ch["t"] + res[0:c]
                    ch["xp"] = res[c:2 * c]
                else:
                    ch["t"] = ch["t"] + _dot(ch["t"].astype(BF16), xb)
        for ch in chains:
            z = jnp.concatenate([ch["akv"].astype(BF16), ch["atm"]], axis=1)
            ch["tz"] = _dot(ch["t"].astype(BF16), z)
        for i in range(0, len(chains), 2):
            c0, c1 = chains[i], chains[i + 1]
            ls = slice(c0["p"] * LANES, (c0["p"] + 1) * LANES)
            w1_s[pl.ds(c0["r0"], c), ls] = jnp.where(lo, c0["tz"][:, 0:LANES], c1["tz"][:, 0:LANES])
            w2_s[pl.ds(c0["r0"], c), ls] = (c0["tz"][:, LANES:2 * LANES] + c1["tz"][:, LANES:2 * LANES]).astype(BF16)
        return carry

    lax.fori_loop(0, tm // (2 * c), solve, 0)

    def scan(ci_, carry):
        r0 = pl.multiple_of(ci_ * c, c)
        pts, rps, bks, vbs = [], [], [], []
        for p in range(npair):
            ls = slice(p * LANES, (p + 1) * LANES)
            pt = st_ref[p]
            rw = jnp.concatenate([rt_s[pl.ds(r0, c), ls], w2_s[pl.ds(r0, c), ls]], axis=0)
            pts.append(pt)
            rps.append(_dot_nt(rw, pt.astype(BF16)))
        uvs = []
        for p in range(npair):
            ls = slice(p * LANES, (p + 1) * LANES)
            u = w1_s[pl.ds(r0, c), ls] + rps[p][c:2 * c]
            vb = vb_s[pl.ds(r0, c), ls]
            uvs.append(jnp.concatenate([u.astype(BF16), vb], axis=0))
            bks.append(jnp.concatenate([bt_s[pl.ds(r0, c), ls], kt_s[pl.ds(r0, c), ls]], axis=0))
        for p in range(npair):
            ls = slice(p * LANES, (p + 1) * LANES)
            upd = _dot_tn(uvs[p], bks[p])
            eg_last = eg_s[pl.ds(pl.multiple_of(r0 + c - 8, 8), 8), ls][7:8, :]
            st_ref[p] = jnp.where(bdmask, (pts[p] + upd) * eg_last, 0.0)
        for p in range(npair):
            ls = slice(p * LANES, (p + 1) * LANES)
            y0 = _dot(tops_s[pl.ds(r0, c), (2 * p) * LANES:(2 * p + 1) * LANES], uvs[p])
            y1 = _dot(tops_s[pl.ds(r0, c), (2 * p + 1) * LANES:(2 * p + 2) * LANES], uvs[p])
            y_s[pl.ds(r0, c), ls] = rps[p][0:c] + jnp.where(lo, y0, y1)
        return carry

    lax.fori_loop(0, tm // c, scan, 0)

    y = y_s[...]
    inv = 1.0 / RWKV_HEAD
    mean = head_sum(y, 2) * inv
    yc = y - mean
    var = head_sum(yc * yc, 2) * inv
    yn = yc * lax.rsqrt(var + RWKV_GN_EPS) * lng_ref[...] + lnb_ref[...]
    bonus = head_sum(r * k * rk_ref[...], 2) * v
    o_ref[0] = (yn + bonus) * g


def _rwkv_call(proj, mu, w0, w2, a0, a2, g2, k_k, k_a, r_k, ln_g, ln_b, tm=256):
    bsz, t_len, _ = proj.shape
    w = BRANCH_W
    w2p = jnp.zeros((LANES, w), F32).at[:64].set(w2).astype(BF16)
    a2p = jnp.zeros((LANES, w), F32).at[64:].set(a2).astype(BF16)
    vec = lambda n: pl.BlockSpec((1, n), lambda b, t: (0, 0))
    mat = lambda n: pl.BlockSpec((n, w), lambda b, t: (0, 0))
    return pl.pallas_call(
        functools.partial(_rwkv_kernel, tm=tm),
        grid=(bsz, t_len // tm),
        in_specs=[pl.BlockSpec((1, tm, RWKV_COLS), lambda b, t: (b, t, 0)),
                  vec(RWKV_COLS), vec(w), mat(LANES), vec(w), mat(LANES), mat(LANES),
                  vec(w), vec(w), vec(w), vec(w), vec(w)],
        out_specs=pl.BlockSpec((1, tm, w), lambda b, t: (b, t, 0)),
        out_shape=jax.ShapeDtypeStruct((bsz, t_len, w), F32),
        scratch_shapes=[pltpu.VMEM((8, RWKV_COLS), F32), pltpu.VMEM((w // LANES, LANES, LANES), F32)]
        + [pltpu.VMEM((tm, w), BF16) for _ in range(6)] + [pltpu.VMEM((tm, w), F32) for _ in range(3)]
        + [pltpu.VMEM((tm, 2 * w), BF16)],
        compiler_params=_cparams(("parallel", "arbitrary")),
        name="rwkv7",
    )(proj, mu.reshape(1, -1), w0.reshape(1, w), w2p, a0.reshape(1, w), a2p, g2.astype(BF16),
      k_k.reshape(1, w), k_a.reshape(1, w), r_k.reshape(1, w), ln_g.reshape(1, w), ln_b.reshape(1, w))


def _foxf_kernel(fl_ref, fb_ref, fcum_ref, carry_ref, *, tm):
    @pl.when(pl.program_id(1) == 0)
    def _():
        carry_ref[...] = jnp.zeros_like(carry_ref)

    lf = _log_sigmoid(fl_ref[0] + fb_ref[...])
    cum = _dot_split_rhs(_tri(tm).astype(BF16), lf, 3) + carry_ref[0:1, :]
    carry_ref[0:1, :] = cum[tm - 1:tm, :]
    src = lax.broadcasted_iota(jnp.int32, (LANES, LANES), 0)
    for h in range(FOX_HEADS):
        fcum_ref[0, h] = _dot_split_lhs(cum, jnp.where(src == h, 1.0, 0.0).astype(BF16), 3)


def _foxf_call(fl, f_bias, tm=512):
    bsz, t_len, _ = fl.shape
    fb = jnp.zeros((1, LANES), F32).at[0, :FOX_HEADS].set(f_bias)
    return pl.pallas_call(
        functools.partial(_foxf_kernel, tm=tm),
        grid=(bsz, t_len // tm),
        in_specs=[pl.BlockSpec((1, tm, LANES), lambda b, t: (b, t, 0)),
                  pl.BlockSpec((1, LANES), lambda b, t: (0, 0))],
        out_specs=pl.BlockSpec((1, FOX_HEADS, tm, LANES), lambda b, t: (b, 0, t, 0)),
        out_shape=jax.ShapeDtypeStruct((bsz, FOX_HEADS, t_len, LANES), F32),
        scratch_shapes=[pltpu.VMEM((8, LANES), F32)],
        compiler_params=_cparams(("parallel", "arbitrary")),
        name="fox_cumf",
    )(fl, fb)


LOG2E = 1.4426950408889634


def _fox_kernel(q_ref, k_ref, v_ref, fcum_ref, o_ref, m_s, l_s, acc_s, qm_s, *, tq):
    qi = pl.program_id(2)
    lane = lax.broadcasted_iota(jnp.int32, (1, LANES), 1)
    m_s[...] = jnp.full_like(m_s, -jnp.inf)
    l_s[...] = jnp.zeros_like(l_s)
    acc_s[...] = jnp.zeros_like(acc_s)
    q = q_ref[0] * (FOX_DH ** -0.5 * LOG2E)
    qm_s[0] = jnp.where(lane < FOX_DH, q, 0.0).astype(BF16)
    qm_s[1] = jnp.where(lane < FOX_DH, 0.0, q).astype(BF16)
    f_0 = [fcum_ref[0, h, pl.ds(pl.multiple_of(qi * tq, tq), 8), :][0:1, :] for h in range(2)]

    def step(ki, masked):
        rows = pl.ds(pl.multiple_of(ki * tq, tq), tq)
        kb = k_ref[0, rows, :]
        vb = v_ref[0, rows, :]
        sts = [_dot_nt(kb, qm_s[h]) for h in range(2)]
        ps, alphas = [], []
        for h in range(2):
            bias = (fcum_ref[0, h, rows, :] - f_0[h]) * LOG2E
            s = jnp.concatenate([sts[h][:, j * LANES:(j + 1) * LANES] - bias for j in range(tq // LANES)], axis=1)
            if masked:
                krow = lax.broadcasted_iota(jnp.int32, (tq, tq), 0)
                qcol = lax.broadcasted_iota(jnp.int32, (tq, tq), 1)
                s = jnp.where(krow <= qcol, s, -1e30)
            m_old = m_s[h]
            m_new = jnp.maximum(m_old, jnp.max(s, axis=0, keepdims=True))
            alpha = jnp.exp2(m_old - m_new)
            p = jnp.exp2(s - m_new)
            l_s[h] = alpha * l_s[h] + jnp.sum(p, axis=0, keepdims=True)
            m_s[h] = m_new
            ps.append(p.astype(BF16))
            alphas.append(alpha)
        for h in range(2):
            acc_s[h] = alphas[h] * acc_s[h] + _dot_tn(vb, ps[h])

    def earlier(ki, carry):
        step(ki, False)
        return carry

    lax.fori_loop(0, qi, earlier, 0)
    step(qi, True)
    o0 = acc_s[0] / l_s[0]
    o1 = acc_s[1] / l_s[1]
    chan = lax.broadcasted_iota(jnp.int32, (LANES, 1), 0)
    o_ref[0] = jnp.where(chan < FOX_DH, o0, o1).T


def _fox_call(q, kv, fcum, tq=512):
    bsz, t_len, _ = q.shape
    npairs = FOX_HEADS // 2
    return pl.pallas_call(
        functools.partial(_fox_kernel, tq=tq),
        grid=(bsz, npairs, t_len // tq),
        in_specs=[
            pl.BlockSpec((1, tq, LANES), lambda b, p, i: (b, i, p)),
            pl.BlockSpec((1, t_len, LANES), lambda b, p, i: (b, 0, p)),
            pl.BlockSpec((1, t_len, LANES), lambda b, p, i: (b, 0, npairs + p)),
            pl.BlockSpec((1, 2, t_len, LANES), lambda b, p, i: (b, p, 0, 0)),
        ],
        out_specs=pl.BlockSpec((1, tq, LANES), lambda b, p, i: (b, i, p)),
        out_shape=jax.ShapeDtypeStruct((bsz, t_len, BRANCH_W), F32),
        scratch_shapes=[pltpu.VMEM((2, 1, tq), F32), pltpu.VMEM((2, 1, tq), F32), pltpu.VMEM((2, LANES, tq), F32),
                        pltpu.VMEM((2, tq, LANES), BF16)],
        compiler_params=_cparams(("parallel", "parallel", "arbitrary")),
        name="fox_attn",
    )(q, kv, kv, fcum)


def _merge_kernel(x_ref, h_ref, ba_ref, bb_ref, bc_ref, bd_ref, mod_ref, wg_ref, bg_ref, wbr_ref, wo_ref,
                  l1g_ref, l1b_ref, rw_ref, rb_ref, x1_ref, h2_ref, cmb_ref):
    hb = h_ref[0]
    merged = None
    for n, br_ref in enumerate((ba_ref, bb_ref, bc_ref, bd_ref)):
        gate = _sigmoid(_dot(hb, wg_ref[n]) + bg_ref[n:n + 1, :])
        term = gate * _dot(br_ref[0].astype(BF16), wbr_ref[n])
        merged = term if merged is None else merged + term
    y = _dot(merged.astype(BF16), wo_ref[...])
    gt1 = mod_ref[0, 2:3, :]
    x1 = _ln(DN_ALPHA * x_ref[0] + gt1 * y) * l1g_ref[...] + l1b_ref[...]
    x1_ref[0] = x1
    h2 = _ln(x1) * (1.0 + mod_ref[0, 4:5, :]) + mod_ref[0, 3:4, :]
    h2_ref[0] = h2.astype(BF16)

    tm = x1.shape[0]
    lane = lax.broadcasted_iota(jnp.int32, (tm, LANES), 1)
    lanef = lane.astype(F32)
    neg = -jnp.inf
    h_hi, h_lo = _split_bf16(h2, 2)
    r_hi, r_lo = _split_bf16(rw_ref[...], 2)
    logits = jnp.where(lane < N_EXPERTS, _dot(h_hi, r_hi) + (_dot(h_hi, r_lo) + _dot(h_lo, r_hi)), neg)
    ex = jnp.exp(logits - jnp.max(logits, axis=-1, keepdims=True))
    probs = ex / jnp.sum(ex, axis=-1, keepdims=True)
    sel = probs + rb_ref[...]
    grp = lane // 4

    def top2(vals):
        m1 = jnp.max(vals, axis=-1, keepdims=True)
        i1 = jnp.min(jnp.where(vals == m1, lanef, 999.0), axis=-1, keepdims=True)
        rest = jnp.where(lanef == i1, neg, vals)
        m2 = jnp.max(rest, axis=-1, keepdims=True)
        i2 = jnp.min(jnp.where(rest == m2, lanef, 999.0), axis=-1, keepdims=True)
        return m1, i1, m2, i2

    best = None
    for gidx in range(4):
        m1, _, m2, _ = top2(jnp.where(grp == gidx, sel, neg))
        score = m1 + m2
        if best is None:
            best, gsel = score, jnp.zeros_like(score, dtype=jnp.int32)
        else:
            better = score > best
            gsel = jnp.where(better, gidx, gsel)
            best = jnp.where(better, score, best)
    _, i1, _, i2 = top2(jnp.where(grp == gsel, sel, neg))
    w1 = jnp.sum(jnp.where(lanef == i1, probs, 0.0), axis=-1, keepdims=True)
    w2 = jnp.sum(jnp.where(lanef == i2, probs, 0.0), axis=-1, keepdims=True)
    cmb_ref[0] = (jnp.where(lanef == i1, w1, 0.0) + jnp.where(lanef == i2, w2, 0.0)) / (w1 + w2)


def _merge_call(x, h, branches, mod, wg, bg, wbr, wo, l1g, l1b, rw, rb, tm=512):
    bsz, t_len, _ = x.shape
    tok = lambda w: pl.BlockSpec((1, tm, w), lambda b, t: (b, t, 0))
    const = lambda shape: pl.BlockSpec(shape, lambda b, t: (0,) * len(shape), pipeline_mode=pl.Buffered(1))
    return pl.pallas_call(
        _merge_kernel,
        grid=(bsz, t_len // tm),
        in_specs=[tok(D_MODEL), tok(D_MODEL)] + [tok(BRANCH_W)] * 4
        + [pl.BlockSpec((1, N_ADA, D_MODEL), lambda b, t: (b, 0, 0)),
           const((4, D_MODEL, D_MODEL)), const((4, D_MODEL)), const((4, BRANCH_W, D_MODEL)), const((D_MODEL, D_MODEL)),
           const((1, D_MODEL)), const((1, D_MODEL)), const((D_MODEL, LANES)), const((1, LANES))],
        out_specs=[tok(D_MODEL), tok(D_MODEL), tok(LANES)],
        out_shape=[jax.ShapeDtypeStruct((bsz, t_len, D_MODEL), F32),
                   jax.ShapeDtypeStruct((bsz, t_len, D_MODEL), BF16),
                   jax.ShapeDtypeStruct((bsz, t_len, LANES), F32)],
        compiler_params=_cparams(("parallel", "parallel")),
        name="merge",
    )(x, h, *branches, mod, wg, bg, wbr, wo, l1g, l1b, rw, rb)


def _moe_kernel(h_ref, cmb_ref, x_ref, mod_ref, wg_ref, wu_ref, wd_ref, l2g_ref, l2b_ref, o_ref, acc_ref):
    e = pl.program_id(2)

    @pl.when(e == 0)
    def _():
        acc_ref[...] = jnp.zeros_like(acc_ref)

    hb = h_ref[0]
    a = _dot(hb, wg_ref[0])
    u = _dot(hb, wu_ref[0])
    he = (a * _sigmoid(a) * u).astype(BF16)
    cmb = cmb_ref[0]
    lane = lax.broadcasted_iota(jnp.int32, cmb.shape, 1)
    ce = jnp.sum(jnp.where(lane == e, cmb, 0.0), axis=-1, keepdims=True)
    acc_ref[...] += ce * _dot(he, wd_ref[0])

    @pl.when(e == N_EXPERTS - 1)
    def _():
        gt2 = mod_ref[0, 5:6, :]
        o_ref[0] = _ln(DN_ALPHA * x_ref[0] + gt2 * acc_ref[...]) * l2g_ref[...] + l2b_ref[...]


def _moe_call(h2, cmb, x1, mod, wg, wu, wd, l2g, l2b, tm=1024):
    bsz, t_len, _ = x1.shape
    tok = lambda w: pl.BlockSpec((1, tm, w), lambda b, t, e: (b, t, 0))
    return pl.pallas_call(
        _moe_kernel,
        grid=(bsz, t_len // tm, N_EXPERTS),
        in_specs=[tok(D_MODEL), tok(LANES), tok(D_MODEL),
                  pl.BlockSpec((1, N_ADA, D_MODEL), lambda b, t, e: (b, 0, 0)),
                  pl.BlockSpec((1, D_MODEL, D_FF_EXPERT), lambda b, t, e: (e, 0, 0)),
                  pl.BlockSpec((1, D_MODEL, D_FF_EXPERT), lambda b, t, e: (e, 0, 0)),
                  pl.BlockSpec((1, D_FF_EXPERT, D_MODEL), lambda b, t, e: (e, 0, 0)),
                  pl.BlockSpec((1, D_MODEL), lambda b, t, e: (0, 0)),
                  pl.BlockSpec((1, D_MODEL), lambda b, t, e: (0, 0))],
        out_specs=tok(D_MODEL),
        out_shape=jax.ShapeDtypeStruct((bsz, t_len, D_MODEL), F32),
        scratch_shapes=[pltpu.VMEM((tm, D_MODEL), F32)],
        compiler_params=_cparams(("parallel", "parallel", "arbitrary")),
        name="moe",
    )(h2, cmb, x1, mod, wg, wu, wd, l2g, l2b)


def _pad_cols(w, n):
    return jnp.pad(w, ((0, 0), (0, n - w.shape[1])))


def kernel(x, c, ada_w, ada_b, w_in, gla_alpha_up, gla_alpha_b, gla_norm_g, rwkv_mu, rwkv_w0, rwkv_w2, rwkv_a0, rwkv_a2, rwkv_g2, rwkv_k_k, rwkv_k_a, rwkv_r_k, rwkv_ln_g, rwkv_ln_b, fox_f_bias, hgrn_lb_logits, hgrn_norm_g, w_br, w_gate, b_gate, w_o, ln1_g, ln1_b, router_w, router_b, exp_w_gate, exp_w_up, exp_w_down, ln2_g, ln2_b):
    mod_all = _ada_call(c, ada_w, ada_b)
    rw = _pad_cols(router_w, LANES)
    rb = _pad_cols(router_b.reshape(1, N_EXPERTS), LANES)
    o1, o2, o3 = GLA_COLS, GLA_COLS + RWKV_COLS, GLA_COLS + RWKV_COLS + FOX_COLS
    for i in range(DEPTH):
        mod = mod_all[i]
        wi = w_in[i]
        wa = _pad_cols(wi[:, :o1], PAD_COLS).astype(BF16)
        wb = wi[:, o1:o2].astype(BF16)
        wc = _pad_cols(wi[:, o2:o3], PAD_COLS).astype(BF16)
        wd = wi[:, o3:].astype(BF16)
        h, pa, pb, pcq, pckv, pcf, pd = _inproj_call(x, mod, wa, wb, wc, wd)
        br_a = _gla_call(pa, gla_alpha_up[i], gla_alpha_b[i], gla_norm_g[i])
        br_b = _rwkv_call(pb, rwkv_mu[i], rwkv_w0[i], rwkv_w2[i], rwkv_a0[i], rwkv_a2[i], rwkv_g2[i],
                          rwkv_k_k[i], rwkv_k_a[i], rwkv_r_k[i], rwkv_ln_g[i], rwkv_ln_b[i])
        br_c = _fox_call(pcq, pckv, _foxf_call(pcf, fox_f_bias[i]))
        br_d = _hgrn_call(pd, hgrn_lb_logits, hgrn_norm_g[i], i)
        x1, h2, cmb = _merge_call(
            x, h, (br_a, br_b, br_c, br_d), mod, w_gate[i].astype(BF16), b_gate[i], w_br[i].astype(BF16),
            w_o[i].astype(BF16), ln1_g[i].reshape(1, -1), ln1_b[i].reshape(1, -1), rw, rb)
        x = _moe_call(h2, cmb, x1, mod, exp_w_gate[i].astype(BF16), exp_w_up[i].astype(BF16),
                      exp_w_down[i].astype(BF16), ln2_g[i].reshape(1, -1), ln2_b[i].reshape(1, -1))
    return x
```

```python
import functools

import jax
import jax.numpy as jnp
from jax import lax
from jax.experimental import pallas as pl
from jax.experimental.pallas import tpu as pltpu

F32 = jnp.float32
BF16 = jnp.bfloat16
HI = lax.Precision.HIGHEST

D_MODEL = 1024
DEPTH = 2
N_ADA = 6
BRANCH_W = 512
GLA_HEADS = 4
GLA_DK = 64
GLA_KW = 256
GLA_GATE_RANK = 16
GLA_LOGIT_NORM = 16.0
RWKV_HEAD = 64
RWKV_COLS = 1792
RWKV_GN_EPS = 64e-5
FOX_HEADS = 8
FOX_DH = 64
HGRN_KW = 512
N_EXPERTS = 16
N_GROUPS = 4
D_FF_EXPERT = 512
MOE_SEG = 128
DN_ALPHA = (2.0 * DEPTH) ** 0.25
GLA_COLS = 1552
FOX_COLS = 1544
HGRN_COLS = 2048
PAD_COLS = 1664

LANES = 128
CHUNK = 64
SUB = 16
FACTORED_MAX_DECAY = 80.0
FACTORED_MAX_ABS = 1e3
VMEM_LIMIT = 56 * 1024 * 1024


def _cparams(sem):
    return pltpu.CompilerParams(dimension_semantics=sem, vmem_limit_bytes=VMEM_LIMIT)


def _sigmoid(x):
    return 1.0 / (1.0 + jnp.exp(-x))


def _log_sigmoid(x):
    return jnp.minimum(x, 0.0) - jnp.log(1.0 + jnp.exp(-jnp.abs(x)))


def _ln(x, eps=1e-5):
    mu = jnp.mean(x, axis=-1, keepdims=True)
    xc = x - mu
    var = jnp.mean(xc * xc, axis=-1, keepdims=True)
    return xc * lax.rsqrt(var + eps)


def _tri(n, strict=False):
    r = lax.broadcasted_iota(jnp.int32, (n, n), 0)
    c = lax.broadcasted_iota(jnp.int32, (n, n), 1)
    return jnp.where((c < r) if strict else (c <= r), 1.0, 0.0).astype(F32)


def _dot(a, b, prec=None):
    return jnp.dot(a, b, preferred_element_type=F32, precision=prec)


def _dot_nt(a, b, prec=None):
    return lax.dot_general(a, b, (((1,), (1,)), ((), ())), preferred_element_type=F32, precision=prec)


def _alternate(*gens):
    live = list(gens)
    while live:
        for gen in list(live):
            if next(gen, "done") == "done":
                live.remove(gen)


def _dot_tn(a, b, prec=None):
    return lax.dot_general(a, b, (((0,), (0,)), ((), ())), preferred_element_type=F32, precision=prec)


def _ada_kernel(c_ref, w_ref, b_ref, o_ref):
    c = c_ref[...]
    cond = c * _sigmoid(c)
    o_ref[0] = _dot(cond, w_ref[0], HI) + b_ref[0]


def _ada_call(c, ada_w, ada_b):
    bsz = c.shape[0]
    n = N_ADA * D_MODEL
    bn = 1536
    out = pl.pallas_call(
        _ada_kernel,
        grid=(DEPTH, n // bn),
        in_specs=[
            pl.BlockSpec((bsz, D_MODEL), lambda i, j: (0, 0)),
            pl.BlockSpec((1, D_MODEL, bn), lambda i, j: (i, 0, j)),
            pl.BlockSpec((1, 1, bn), lambda i, j: (i, 0, j)),
        ],
        out_specs=pl.BlockSpec((1, bsz, bn), lambda i, j: (i, 0, j)),
        out_shape=jax.ShapeDtypeStruct((DEPTH, bsz, n), F32),
        compiler_params=_cparams(("parallel", "parallel")),
        name="ada_mod",
    )(c, ada_w, ada_b.reshape(DEPTH, 1, n))
    return out.reshape(DEPTH, bsz, N_ADA, D_MODEL)


def _inproj_kernel(x_ref, mod_ref, wa_ref, wb_ref, wc_ref, wd_ref,
                   h_ref, oa_ref, ob_ref, ocq_ref, ockv_ref, ocf_ref, od_ref):
    x = x_ref[0]
    sh = mod_ref[0, 0:1, :]
    sc = mod_ref[0, 1:2, :]
    hb = (_ln(x) * (1.0 + sc) + sh).astype(BF16)
    h_ref[0] = hb
    oa_ref[0] = _dot(hb, wa_ref[...])
    ob_ref[0] = _dot(hb, wb_ref[...])
    oc = _dot(hb, wc_ref[...])
    w = BRANCH_W
    ocq_ref[0] = oc[:, 0:w]
    ockv_ref[0] = oc[:, w:3 * w].astype(BF16)
    ocf_ref[0] = oc[:, 3 * w:3 * w + LANES]
    od_ref[0] = _dot(hb, wd_ref[...])


def _inproj_call(x, mod, wa, wb, wc, wd, tm=256):
    bsz, t_len, _ = x.shape
    widths = (wa.shape[1], wb.shape[1], wc.shape[1], wd.shape[1])
    outs = [(D_MODEL, BF16), (widths[0], F32), (widths[1], F32), (BRANCH_W, F32), (2 * BRANCH_W, BF16),
            (LANES, F32), (widths[3], F32)]

    def wspec(w):
        return pl.BlockSpec((D_MODEL, w), lambda b, t: (0, 0), pipeline_mode=pl.Buffered(1))

    return pl.pallas_call(
        _inproj_kernel,
        grid=(bsz, t_len // tm),
        in_specs=[pl.BlockSpec((1, tm, D_MODEL), lambda b, t: (b, t, 0)),
                  pl.BlockSpec((1, N_ADA, D_MODEL), lambda b, t: (b, 0, 0))] + [wspec(w) for w in widths],
        out_specs=[pl.BlockSpec((1, tm, w), lambda b, t: (b, t, 0)) for w, _ in outs],
        out_shape=[jax.ShapeDtypeStruct((bsz, t_len, w), dt) for w, dt in outs],
        compiler_params=_cparams(("parallel", "parallel")),
        name="inproj",
    )(x, mod, wa, wb, wc, wd)


def _gla_kernel(*refs, mode, layer, hp, qscale, tm, ngrp):
    if mode == "gla":
        q_ref, k_ref, v_ref, gate_ref, al_ref, aup_ref, ab_ref, ng_ref, sel_ref, o_ref = refs[:10]
    else:
        q_ref, k_ref, v_ref, gate_ref, lbl_ref, ng_ref, sel_ref, o_ref = refs[:8]
    st_ref, q_s, k_s, b_s, qe_s, kd_s, pw_s = refs[-7:]
    c = CHUNK
    nb = c // SUB
    dk = LANES // hp

    @pl.when(pl.program_id(1) == 0)
    def _():
        st_ref[...] = jnp.zeros_like(st_ref)

    q = q_ref[0]
    if qscale != 1.0:
        q = q * qscale
    kin = k_ref[0]
    if mode == "gla":
        g = _log_sigmoid(_dot(al_ref[0], aup_ref[...], HI) + ab_ref[...]) / GLA_LOGIT_NORM
        k = kin
    else:
        lg = lbl_ref[...]
        mx = jnp.max(lg, axis=0, keepdims=True)
        ex = jnp.exp(lg - mx)
        p = ex / jnp.sum(ex, axis=0, keepdims=True)
        cum = p[0:1, :]
        for j in range(1, layer + 1):
            cum = cum + p[j:j + 1, :]
        lb = cum - p[0:1, :]
        g = jnp.log(lb + (1.0 - lb) * _sigmoid(kin))
        k = (1.0 - lb) * _sigmoid(-kin)
    tr = lax.broadcasted_iota(jnp.int32, (tm, tm), 0)
    tc = lax.broadcasted_iota(jnp.int32, (tm, tm), 1)
    same = (tr // c) == (tc // c)
    b = _dot_split_rhs(jnp.where((tc <= tr) & same, 1.0, 0.0).astype(BF16), g, 3)
    nch = tm // c
    b_end = jnp.concatenate([jnp.broadcast_to(b[(ci + 1) * c - 1:(ci + 1) * c, :], (c, b.shape[1]))
                             for ci in range(nch)], axis=0)
    qe = (q * jnp.exp(b)).astype(BF16)
    kd = (k * jnp.exp(b_end - b)).astype(BF16)

    lane = lax.broadcasted_iota(jnp.int32, (1, LANES), 1)
    hmask = [((lane >= h * dk) & (lane < (h + 1) * dk)) for h in range(hp)]
    row = lax.broadcasted_iota(jnp.int32, (c, c), 0)
    col = lax.broadcasted_iota(jnp.int32, (c, c), 1)
    off_mask = (col // SUB) < (row // SUB)
    diag_mask = ((col // SUB) == (row // SUB)) & (col <= row)
    ng = ng_ref[...]

    def head_lanes(x, h):
        return x if hp == 1 else jnp.where(hmask[h], x, jnp.zeros_like(x))

    def finish(rows, hv, o):
        o = o * lax.rsqrt(jnp.mean(o * o, axis=-1, keepdims=True) + 1e-6) * ng
        gt = gate_ref[0, rows, hv * LANES:(hv + 1) * LANES]
        o_ref[0, rows, hv * LANES:(hv + 1) * LANES] = o * (gt * _sigmoid(gt))

    def intra_exact(rows):
        qhs, khats = {}, []
        for gi in range(ngrp):
            gl = slice(gi * LANES, (gi + 1) * LANES)
            qc = q_s[rows, gl]
            kc = k_s[rows, gl]
            bc = b_s[rows, gl]
            rblk = jnp.concatenate([jnp.broadcast_to(bc[m * SUB:m * SUB + 1, :], (SUB, LANES)) for m in range(nb)],
                                   axis=0)
            qh = (qc * jnp.exp(bc - rblk)).astype(BF16)
            for h in range(hp):
                qhs[gi * hp + h] = head_lanes(qh, h)
            khats.append([None] + [(kc * jnp.exp(jnp.minimum(bc[m * SUB:m * SUB + 1, :] - bc, 0.0))).astype(BF16)
                                   for m in range(1, nb)])
            for jj in range(SUB):
                kb = jnp.concatenate([jnp.broadcast_to(kc[m * SUB + jj:m * SUB + jj + 1, :], (SUB, LANES))
                                      for m in range(nb)], axis=0)
                bb = jnp.concatenate([jnp.broadcast_to(bc[m * SUB + jj:m * SUB + jj + 1, :], (SUB, LANES))
                                      for m in range(nb)], axis=0)
                pw_s[gi * c:(gi + 1) * c, jj * LANES:(jj + 1) * LANES] = (
                    qc * kb * jnp.exp(jnp.minimum(bc - bb, 0.0))).astype(BF16)
        a_offs = {}
        for hv in range(ngrp * hp):
            a_offs[hv] = jnp.concatenate(
                [jnp.zeros((SUB, c), F32)]
                + [_dot_nt(qhs[hv][m * SUB:(m + 1) * SUB, :], khats[hv // hp][m]) for m in range(1, nb)], axis=0)
        a_diags = [_dot(pw_s[...], sel_ref[h]) for h in range(hp)]
        out = {}
        for hv in range(ngrp * hp):
            gi, h = hv // hp, hv % hp
            out[hv] = jnp.where(off_mask, a_offs[hv], jnp.where(diag_mask, a_diags[h][gi * c:(gi + 1) * c], 0.0))
        return out

    def chunk(ci, carry):
        r0 = pl.multiple_of(ci * c, c)
        rows = pl.ds(r0, c)
        sts, o_int, upds, vbs = [], {}, {}, {}
        for gi in range(ngrp):
            gl = slice(gi * LANES, (gi + 1) * LANES)
            sts.append(st_ref[gi])
            stb = sts[gi].astype(BF16)
            for h in range(hp):
                hv = gi * hp + h
                vbs[hv] = v_ref[0, rows, hv * LANES:(hv + 1) * LANES].astype(BF16)
                o_int[hv] = _dot_nt(head_lanes(qe_s[rows, gl], h), stb)
                upds[hv] = _dot_tn(vbs[hv], kd_s[rows, gl])
        attn = intra_exact(rows)
        for hv in range(ngrp * hp):
            finish(rows, hv, o_int[hv] + _dot(attn[hv].astype(BF16), vbs[hv]))
        for gi in range(ngrp):
            gl = slice(gi * LANES, (gi + 1) * LANES)
            b_last = b_s[pl.ds(pl.multiple_of(r0 + c - 8, 8), 8), gl][7:8, :]
            new_st = sts[gi] * jnp.exp(b_last)
            for h in range(hp):
                new_st = new_st + head_lanes(upds[gi * hp + h], h)
            st_ref[gi] = new_st
        return carry

    small_decay = jnp.max(jnp.maximum(-b * (1.0 / FACTORED_MAX_DECAY),
                                      jnp.maximum(jnp.abs(q), jnp.abs(k)) * (1.0 / FACTORED_MAX_ABS))) < 1.0

    @pl.when(small_decay)
    def _():
        nch = tm // c
        heads = [(ci, hv) for ci in range(nch) for hv in range(ngrp * hp)]
        b0 = jnp.concatenate([jnp.broadcast_to(b[ci * c:ci * c + 1, :], (c, b.shape[1])) for ci in range(nch)], axis=0)
        bz = b - b0
        qz = (q * jnp.exp(bz)).astype(BF16)
        kz = (k * jnp.exp(-bz)).astype(BF16)
        sl = lambda ci, gi: (slice(ci * c, (ci + 1) * c), slice(gi * LANES, (gi + 1) * LANES))
        vbs, upds, attn, o_intra, o_int = {}, {}, {}, {}, {}
        for ci, hv in heads:
            vbs[ci, hv] = v_ref[0, ci * c:(ci + 1) * c, hv * LANES:(hv + 1) * LANES].astype(BF16)
            upds[ci, hv] = _dot_tn(vbs[ci, hv], kd[sl(ci, hv // hp)])
        for ci, hv in heads:
            attn[ci, hv] = jnp.where(col <= row, _dot_nt(head_lanes(qz[sl(ci, hv // hp)], hv % hp),
                                                           kz[sl(ci, hv // hp)]), 0.0).astype(BF16)
        for ci, hv in heads:
            o_intra[ci, hv] = _dot(attn[ci, hv], vbs[ci, hv])
        sts = [[st_ref[gi]] for gi in range(ngrp)]
        for ci in range(nch):
            for gi in range(ngrp):
                new_st = sts[gi][ci] * jnp.exp(b[(ci + 1) * c - 1:(ci + 1) * c, gi * LANES:(gi + 1) * LANES])
                for h in range(hp):
                    new_st = new_st + head_lanes(upds[ci, gi * hp + h], h)
                sts[gi].append(new_st)
        for gi in range(ngrp):
            st_ref[gi] = sts[gi][nch]
        for ci, hv in heads:
            o_int[ci, hv] = _dot_nt(head_lanes(qe[sl(ci, hv // hp)], hv % hp), sts[hv // hp][ci].astype(BF16))
        for ci, hv in heads:
            finish(slice(ci * c, (ci + 1) * c), hv, o_int[ci, hv] + o_intra[ci, hv])

    @pl.when(jnp.logical_not(small_decay))
    def _():
        q_s[...] = q
        k_s[...] = k
        b_s[...] = b
        qe_s[...] = qe
        kd_s[...] = kd
        lax.fori_loop(0, nch, chunk, 0)


def _pair_select(hp):
    dk = LANES // hp
    r = jnp.arange(SUB * LANES)
    jj = r // LANES
    head = (r % LANES) // dk
    cj = jnp.arange(CHUNK) % SUB
    sel = (jj[None, :, None] == cj[None, None, :]) & (head[None, :, None] == jnp.arange(hp)[:, None, None])
    return sel.astype(BF16)


def _gla_scratch(tm, ngrp):
    w = ngrp * LANES
    return ([pltpu.VMEM((ngrp, LANES, LANES), F32)] + [pltpu.VMEM((tm, w), F32) for _ in range(3)]
            + [pltpu.VMEM((tm, w), BF16) for _ in range(2)] + [pltpu.VMEM((ngrp * CHUNK, SUB * LANES), BF16)])


def _gla_call(proj, alpha_up, alpha_b, norm_g, tm=256):
    bsz, t_len, _ = proj.shape
    hp = 2
    ngrp = GLA_HEADS // hp
    aup = jnp.zeros((LANES, GLA_KW), F32).at[:GLA_GATE_RANK].set(alpha_up)
    blk = lambda w, j: pl.BlockSpec((1, tm, w), lambda b, t: (b, t, j))
    const = lambda shape: pl.BlockSpec(shape, lambda b, t: (0,) * len(shape))
    kern = functools.partial(_gla_kernel, mode="gla", layer=0, hp=hp, qscale=GLA_DK ** -0.5, tm=tm, ngrp=ngrp)
    return pl.pallas_call(
        kern,
        grid=(bsz, t_len // tm),
        in_specs=[blk(GLA_KW, 0), blk(GLA_KW, 1), blk(BRANCH_W, 1), blk(BRANCH_W, 2), blk(LANES, 12),
                  const((LANES, GLA_KW)), const((1, GLA_KW)), const((1, LANES)), const((hp, SUB * LANES, CHUNK))],
        out_specs=blk(BRANCH_W, 0),
        out_shape=jax.ShapeDtypeStruct((bsz, t_len, BRANCH_W), F32),
        scratch_shapes=_gla_scratch(tm, ngrp),
        compiler_params=_cparams(("parallel", "arbitrary")),
        name="gla",
    )(proj, proj, proj, proj, proj, aup, alpha_b.reshape(1, GLA_KW), norm_g.reshape(1, LANES), _pair_select(hp))


def _hgrn_call(proj, lb_logits, norm_g, layer, tm=256):
    bsz, t_len, _ = proj.shape
    ngrp = HGRN_KW // LANES
    blk = lambda j: pl.BlockSpec((1, tm, HGRN_KW), lambda b, t: (b, t, j))
    const = lambda shape: pl.BlockSpec(shape, lambda b, t: (0,) * len(shape))
    kern = functools.partial(_gla_kernel, mode="hgrn", layer=layer, hp=1, qscale=1.0, tm=tm, ngrp=ngrp)
    return pl.pallas_call(
        kern,
        grid=(bsz, t_len // tm),
        in_specs=[blk(0), blk(1), blk(2), blk(3), const((DEPTH, HGRN_KW)), const((1, LANES)),
                  const((1, SUB * LANES, CHUNK))],
        out_specs=blk(0),
        out_shape=jax.ShapeDtypeStruct((bsz, t_len, BRANCH_W), F32),
        scratch_shapes=_gla_scratch(tm, ngrp),
        compiler_params=_cparams(("parallel", "arbitrary")),
        name="hgrn",
    )(proj, proj, proj, proj, lb_logits, norm_g.reshape(1, LANES), _pair_select(1))


def _split_bf16(x, n):
    parts, rest = [], x
    for i in range(n):
        part = rest.astype(BF16)
        parts.append(part)
        if i + 1 < n:
            rest = rest - part.astype(F32)
    return parts


def _dot_split_rhs(a01, x, n):
    out = None
    for part in _split_bf16(x, n):
        term = _dot(a01, part)
        out = term if out is None else out + term
    return out


def _dot_split_lhs(x, b01, n):
    out = None
    for part in _split_bf16(x, n):
        term = _dot(part, b01)
        out = term if out is None else out + term
    return out


def _rwkv_kernel(cols_ref, mu_ref, w0_ref, w2_ref, a0_ref, a2_ref, g2_ref, kk_ref, ka_ref, rk_ref, lng_ref, lnb_ref,
                 o_ref, carry_ref, st_ref, rt_s, at_s, bt_s, kt_s, vb_s, w2_s, eg_s, y_s, w1_s, tops_s, *, tm):
    w = BRANCH_W
    c = CHUNK
    npair = w // LANES

    @pl.when(pl.program_id(1) == 0)
    def _():
        carry_ref[...] = jnp.zeros_like(carry_ref)
        st_ref[...] = jnp.zeros_like(st_ref)

    cols = cols_ref[0]
    rowi = lax.broadcasted_iota(jnp.int32, cols.shape, 0)
    prev = jnp.where(rowi == 0, carry_ref[0:1, :], pltpu.roll(cols, 1, 0))
    carry_ref[0:1, :] = cols[tm - 1:tm, :]
    xs = cols + (prev - cols) * mu_ref[...]
    r = xs[:, 0:w]
    k = xs[:, w:2 * w]
    v = xs[:, 2 * w:3 * w]
    wl = xs[:, 3 * w:3 * w + LANES]
    gl = xs[:, 3 * w + LANES:3 * w + 2 * LANES]
    w_raw = _log_sigmoid(w0_ref[...] + _dot(jnp.tanh(wl).astype(BF16), w2_ref[...])) - 0.5
    lw = -jnp.exp(w_raw)
    a = _sigmoid(a0_ref[...] + _dot(wl.astype(BF16), a2_ref[...]))
    g = _dot(_sigmoid(gl).astype(BF16), g2_ref[...])

    hr = lax.broadcasted_iota(jnp.int32, (LANES, LANES), 0) // RWKV_HEAD
    hc = lax.broadcasted_iota(jnp.int32, (LANES, LANES), 1) // RWKV_HEAD
    bdmask = hr == hc
    bd = jnp.where(bdmask, 1.0, 0.0).astype(BF16)

    def head_sum(t, n):
        return jnp.concatenate([_dot_split_lhs(t[:, p * LANES:(p + 1) * LANES], bd, n) for p in range(npair)], axis=1)

    kk = k * kk_ref[...]
    kk = kk / jnp.maximum(jnp.sqrt(head_sum(kk * kk, 3)), 1e-12)
    k = k * (1.0 + (a - 1.0) * ka_ref[...])

    tr = lax.broadcasted_iota(jnp.int32, (tm, tm), 0)
    tc = lax.broadcasted_iota(jnp.int32, (tm, tm), 1)
    tri = jnp.where((tc <= tr) & ((tr // c) == (tc // c)), 1.0, 0.0).astype(BF16)
    gam = _dot_split_rhs(tri, lw, 3)
    eg = jnp.exp(gam)
    ieg = jnp.exp(-gam)
    rt_s[...] = (r * eg).astype(BF16)
    at_s[...] = (-kk * jnp.exp(gam - lw)).astype(BF16)
    bt_s[...] = (kk * a * ieg).astype(BF16)
    kt_s[...] = (k * ieg).astype(BF16)
    vb_s[...] = v.astype(BF16)
    eg_s[...] = eg

    lane = lax.broadcasted_iota(jnp.int32, (1, LANES), 1)
    lo = lane < RWKV_HEAD
    hmask = [lo, jnp.logical_not(lo)]
    row = lax.broadcasted_iota(jnp.int32, (c, LANES), 0)
    col = lax.broadcasted_iota(jnp.int32, (c, LANES), 1)
    top_mask = (col % c) <= row
    ak_mask = (col >= c) & ((col - c) < row)
    r64 = lax.broadcasted_iota(jnp.int32, (c, c), 0)
    c64 = lax.broadcasted_iota(jnp.int32, (c, c), 1)
    strict = c64 < r64
    eye = jnp.where(c64 == r64, 1.0, 0.0).astype(F32)

    def solve(j):
        chains = []
        for cc in range(2):
            r0 = (2 * j + cc) * c
            for p in range(npair):
                ls = slice(p * LANES, (p + 1) * LANES)
                at = at_s[pl.ds(r0, c), ls]
                vb = vb_s[pl.ds(r0, c), ls]
                ra = jnp.concatenate([rt_s[pl.ds(r0, c), ls], at], axis=0)
                bk = jnp.concatenate([bt_s[pl.ds(r0, c), ls], kt_s[pl.ds(r0, c), ls]], axis=0)
                vv = jnp.concatenate([vb, vb], axis=0)
                for h in range(2):
                    x = _dot_nt(jnp.where(hmask[h], ra, jnp.zeros_like(ra)), bk)
                    tops_s[pl.ds(r0, c), (2 * p + h) * LANES:(2 * p + h + 1) * LANES] = (
                        jnp.where(top_mask, x[0:c], 0.0).astype(BF16))
                    chains.append(dict(r0=r0, p=p, h=h, vv=vv, atm=jnp.where(hmask[h], at, jnp.zeros_like(at)),
                                       aab=jnp.where(strict, x[c:2 * c, 0:c], 0.0),
                                       akm=jnp.where(ak_mask, x[c:2 * c], 0.0).astype(BF16)))
            yield
        for ch in chains:
            ab = ch["aab"].astype(BF16)
            ch["xp"] = _dot(ab, ab)
            ch["t"] = eye + ch["aab"]
        yield
        for ch in chains:
            ch["akv"] = _dot(ch["akm"], ch["vv"])
        yield
        for stage in range(5):
            for ch in chains:
                xb = ch["xp"].astype(BF16)
                if stage < 4:
                    res = _dot(jnp.concatenate([ch["t"].astype(BF16), xb], axis=0), xb)
                    ch["t"] = ch["t"] + res[0:c]
                    ch["xp"] = res[c:2 * c]
                else:
                    ch["t"] = ch["t"] + _dot(ch["t"].astype(BF16), xb)
            yield
        for ch in chains:
            z = jnp.concatenate([ch["akv"].astype(BF16), ch["atm"]], axis=1)
            ch["tz"] = _dot(ch["t"].astype(BF16), z)
        for i in range(0, len(chains), 2):
            c0, c1 = chains[i], chains[i + 1]
            ls = slice(c0["p"] * LANES, (c0["p"] + 1) * LANES)
            w1_s[pl.ds(c0["r0"], c), ls] = jnp.where(lo, c0["tz"][:, 0:LANES], c1["tz"][:, 0:LANES])
            w2_s[pl.ds(c0["r0"], c), ls] = (c0["tz"][:, LANES:2 * LANES] + c1["tz"][:, LANES:2 * LANES]).astype(BF16)
        yield

    def scan(ci_):
        r0 = ci_ * c
        pts, rps, bks, vbs = [], [], [], []
        for p in range(npair):
            ls = slice(p * LANES, (p + 1) * LANES)
            pt = st_ref[p]
            rw = jnp.concatenate([rt_s[pl.ds(r0, c), ls], w2_s[pl.ds(r0, c), ls]], axis=0)
            pts.append(pt)
            rps.append(_dot_nt(rw, pt.astype(BF16)))
        yield
        uvs = []
        for p in range(npair):
            ls = slice(p * LANES, (p + 1) * LANES)
            u = w1_s[pl.ds(r0, c), ls] + rps[p][c:2 * c]
            vb = vb_s[pl.ds(r0, c), ls]
            uvs.append(jnp.concatenate([u.astype(BF16), vb], axis=0))
            bks.append(jnp.concatenate([bt_s[pl.ds(r0, c), ls], kt_s[pl.ds(r0, c), ls]], axis=0))
        for p in range(npair):
            ls = slice(p * LANES, (p + 1) * LANES)
            upd = _dot_tn(uvs[p], bks[p])
            eg_last = eg_s[pl.ds(r0 + c - 1, 1), ls]
            st_ref[p] = jnp.where(bdmask, (pts[p] + upd) * eg_last, 0.0)
        yield
        for p in range(npair):
            ls = slice(p * LANES, (p + 1) * LANES)
            y0 = _dot(tops_s[pl.ds(r0, c), (2 * p) * LANES:(2 * p + 1) * LANES], uvs[p])
            y1 = _dot(tops_s[pl.ds(r0, c), (2 * p + 1) * LANES:(2 * p + 2) * LANES], uvs[p])
            y_s[pl.ds(r0, c), ls] = rps[p][0:c] + jnp.where(lo, y0, y1)
        yield

    def scan_pair(j):
        yield from scan(2 * j)
        yield from scan(2 * j + 1)

    npairs_t = tm // (2 * c)
    _alternate(solve(0))
    for j in range(1, npairs_t):
        _alternate(solve(j), scan_pair(j - 1))
    _alternate(scan_pair(npairs_t - 1))

    y = y_s[...]
    inv = 1.0 / RWKV_HEAD
    mean = head_sum(y, 2) * inv
    yc = y - mean
    var = head_sum(yc * yc, 2) * inv
    yn = yc * lax.rsqrt(var + RWKV_GN_EPS) * lng_ref[...] + lnb_ref[...]
    bonus = head_sum(r * k * rk_ref[...], 2) * v
    o_ref[0] = (yn + bonus) * g


def _rwkv_call(proj, mu, w0, w2, a0, a2, g2, k_k, k_a, r_k, ln_g, ln_b, tm=512):
    bsz, t_len, _ = proj.shape
    w = BRANCH_W
    w2p = jnp.zeros((LANES, w), F32).at[:64].set(w2).astype(BF16)
    a2p = jnp.zeros((LANES, w), F32).at[64:].set(a2).astype(BF16)
    vec = lambda n: pl.BlockSpec((1, n), lambda b, t: (0, 0))
    mat = lambda n: pl.BlockSpec((n, w), lambda b, t: (0, 0))
    return pl.pallas_call(
        functools.partial(_rwkv_kernel, tm=tm),
        grid=(bsz, t_len // tm),
        in_specs=[pl.BlockSpec((1, tm, RWKV_COLS), lambda b, t: (b, t, 0)),
                  vec(RWKV_COLS), vec(w), mat(LANES), vec(w), mat(LANES), mat(LANES),
                  vec(w), vec(w), vec(w), vec(w), vec(w)],
        out_specs=pl.BlockSpec((1, tm, w), lambda b, t: (b, t, 0)),
        out_shape=jax.ShapeDtypeStruct((bsz, t_len, w), F32),
        scratch_shapes=[pltpu.VMEM((8, RWKV_COLS), F32), pltpu.VMEM((w // LANES, LANES, LANES), F32)]
        + [pltpu.VMEM((tm, w), BF16) for _ in range(6)] + [pltpu.VMEM((tm, w), F32) for _ in range(3)]
        + [pltpu.VMEM((tm, 2 * w), BF16)],
        compiler_params=_cparams(("parallel", "arbitrary")),
        name="rwkv7",
    )(proj, mu.reshape(1, -1), w0.reshape(1, w), w2p, a0.reshape(1, w), a2p, g2.astype(BF16),
      k_k.reshape(1, w), k_a.reshape(1, w), r_k.reshape(1, w), ln_g.reshape(1, w), ln_b.reshape(1, w))


def _foxf_kernel(fl_ref, fb_ref, fcum_ref, carry_ref, *, tm):
    @pl.when(pl.program_id(1) == 0)
    def _():
        carry_ref[...] = jnp.zeros_like(carry_ref)

    lf = _log_sigmoid(fl_ref[0] + fb_ref[...])
    cum = _dot_split_rhs(_tri(tm).astype(BF16), lf, 3) + carry_ref[0:1, :]
    carry_ref[0:1, :] = cum[tm - 1:tm, :]
    for h in range(FOX_HEADS):
        fcum_ref[0, h] = jnp.broadcast_to(cum[:, h:h + 1], cum.shape)


def _foxf_call(fl, f_bias, tm=512):
    bsz, t_len, _ = fl.shape
    fb = jnp.zeros((1, LANES), F32).at[0, :FOX_HEADS].set(f_bias)
    return pl.pallas_call(
        functools.partial(_foxf_kernel, tm=tm),
        grid=(bsz, t_len // tm),
        in_specs=[pl.BlockSpec((1, tm, LANES), lambda b, t: (b, t, 0)),
                  pl.BlockSpec((1, LANES), lambda b, t: (0, 0))],
        out_specs=pl.BlockSpec((1, FOX_HEADS, tm, LANES), lambda b, t: (b, 0, t, 0)),
        out_shape=jax.ShapeDtypeStruct((bsz, FOX_HEADS, t_len, LANES), F32),
        scratch_shapes=[pltpu.VMEM((8, LANES), F32)],
        compiler_params=_cparams(("parallel", "arbitrary")),
        name="fox_cumf",
    )(fl, fb)


LOG2E = 1.4426950408889634


def _fox_kernel(q_ref, k_ref, v_ref, fcum_ref, o_ref, m_s, l_s, acc_s, qm_s, *, tq):
    qi = pl.program_id(2)
    lane = lax.broadcasted_iota(jnp.int32, (1, LANES), 1)
    m_s[...] = jnp.full_like(m_s, -jnp.inf)
    l_s[...] = jnp.zeros_like(l_s)
    acc_s[...] = jnp.zeros_like(acc_s)
    q = q_ref[0] * (FOX_DH ** -0.5 * LOG2E)
    qm_s[0] = jnp.where(lane < FOX_DH, q, 0.0).astype(BF16)
    qm_s[1] = jnp.where(lane < FOX_DH, 0.0, q).astype(BF16)
    f_0 = [fcum_ref[0, h, pl.ds(pl.multiple_of(qi * tq, tq), 8), :][0:1, :] for h in range(2)]

    def step(ki, masked):
        rows = pl.ds(pl.multiple_of(ki * tq, tq), tq)
        kb = k_ref[0, rows, :]
        vb = v_ref[0, rows, :]
        sts = [_dot_nt(kb, qm_s[h]) for h in range(2)]
        ps, alphas = [], []
        for h in range(2):
            bias = (fcum_ref[0, h, rows, :] - f_0[h]) * LOG2E
            s = jnp.concatenate([sts[h][:, j * LANES:(j + 1) * LANES] - bias for j in range(tq // LANES)], axis=1)
            if masked:
                krow = lax.broadcasted_iota(jnp.int32, (tq, tq), 0)
                qcol = lax.broadcasted_iota(jnp.int32, (tq, tq), 1)
                s = jnp.where(krow <= qcol, s, -1e30)
            m_old = m_s[h]
            m_new = jnp.maximum(m_old, jnp.max(s, axis=0, keepdims=True))
            alpha = jnp.exp2(m_old - m_new)
            p = jnp.exp2(s - m_new)
            l_s[h] = alpha * l_s[h] + jnp.sum(p, axis=0, keepdims=True)
            m_s[h] = m_new
            ps.append(p.astype(BF16))
            alphas.append(alpha)
        for h in range(2):
            acc_s[h] = alphas[h] * acc_s[h] + _dot_tn(vb, ps[h])

    def earlier(ki, carry):
        step(ki, False)
        return carry

    lax.fori_loop(0, qi, earlier, 0)
    step(qi, True)
    o0 = acc_s[0] / l_s[0]
    o1 = acc_s[1] / l_s[1]
    chan = lax.broadcasted_iota(jnp.int32, (LANES, 1), 0)
    o_ref[0] = jnp.where(chan < FOX_DH, o0, o1).T


def _fox_call(q, kv, fcum, tq=512):
    bsz, t_len, _ = q.shape
    npairs = FOX_HEADS // 2
    return pl.pallas_call(
        functools.partial(_fox_kernel, tq=tq),
        grid=(bsz, npairs, t_len // tq),
        in_specs=[
            pl.BlockSpec((1, tq, LANES), lambda b, p, i: (b, i, p)),
            pl.BlockSpec((1, t_len, LANES), lambda b, p, i: (b, 0, p)),
            pl.BlockSpec((1, t_len, LANES), lambda b, p, i: (b, 0, npairs + p)),
            pl.BlockSpec((1, 2, t_len, LANES), lambda b, p, i: (b, p, 0, 0)),
        ],
        out_specs=pl.BlockSpec((1, tq, LANES), lambda b, p, i: (b, i, p)),
        out_shape=jax.ShapeDtypeStruct((bsz, t_len, BRANCH_W), F32),
        scratch_shapes=[pltpu.VMEM((2, 1, tq), F32), pltpu.VMEM((2, 1, tq), F32), pltpu.VMEM((2, LANES, tq), F32),
                        pltpu.VMEM((2, tq, LANES), BF16)],
        compiler_params=_cparams(("parallel", "parallel", "arbitrary")),
        name="fox_attn",
    )(q, kv, kv, fcum)


def _merge_kernel(x_ref, h_ref, ba_ref, bb_ref, bc_ref, bd_ref, mod_ref, wg_ref, bg_ref, wbr_ref, wo_ref,
                  l1g_ref, l1b_ref, rw_ref, rb_ref, x1_ref, h2_ref, cmb_ref):
    tm = x_ref.shape[1]
    half = tm // 2

    def rows_of(r):
        rs = slice(r * half, (r + 1) * half)
        hb = h_ref[0, rs, :]
        merged = None
        for n, br_ref in enumerate((ba_ref, bb_ref, bc_ref, bd_ref)):
            gate = _sigmoid(_dot(hb, wg_ref[n]) + bg_ref[n:n + 1, :])
            term = gate * _dot(br_ref[0, rs, :].astype(BF16), wbr_ref[n])
            merged = term if merged is None else merged + term
        yield
        y = _dot(merged.astype(BF16), wo_ref[...])
        yield
        gt1 = mod_ref[0, 2:3, :]
        x1 = _ln(DN_ALPHA * x_ref[0, rs, :] + gt1 * y) * l1g_ref[...] + l1b_ref[...]
        x1_ref[0, rs, :] = x1
        h2 = _ln(x1) * (1.0 + mod_ref[0, 4:5, :]) + mod_ref[0, 3:4, :]
        h2_ref[0, rs, :] = h2.astype(BF16)

        lane = lax.broadcasted_iota(jnp.int32, (half, LANES), 1)
        lanef = lane.astype(F32)
        neg = -jnp.inf
        h_hi, h_lo = _split_bf16(h2, 2)
        r_hi, r_lo = _split_bf16(rw_ref[...], 2)
        logits = jnp.where(lane < N_EXPERTS, _dot(h_hi, r_hi) + (_dot(h_hi, r_lo) + _dot(h_lo, r_hi)), neg)
        yield
        ex = jnp.exp(logits - jnp.max(logits, axis=-1, keepdims=True))
        probs = ex / jnp.sum(ex, axis=-1, keepdims=True)
        sel = probs + rb_ref[...]
        grp = lane // 4

        def top2(vals):
            m1 = jnp.max(vals, axis=-1, keepdims=True)
            i1 = jnp.min(jnp.where(vals == m1, lanef, 999.0), axis=-1, keepdims=True)
            rest = jnp.where(lanef == i1, neg, vals)
            m2 = jnp.max(rest, axis=-1, keepdims=True)
            i2 = jnp.min(jnp.where(rest == m2, lanef, 999.0), axis=-1, keepdims=True)
            return m1, i1, m2, i2

        best = None
        for gidx in range(4):
            m1, _, m2, _ = top2(jnp.where(grp == gidx, sel, neg))
            score = m1 + m2
            if best is None:
                best, gsel = score, jnp.zeros_like(score, dtype=jnp.int32)
            else:
                better = score > best
                gsel = jnp.where(better, gidx, gsel)
                best = jnp.where(better, score, best)
        _, i1, _, i2 = top2(jnp.where(grp == gsel, sel, neg))
        w1 = jnp.sum(jnp.where(lanef == i1, probs, 0.0), axis=-1, keepdims=True)
        w2 = jnp.sum(jnp.where(lanef == i2, probs, 0.0), axis=-1, keepdims=True)
        cmb_ref[0, rs, :] = ((jnp.where(lanef == i1, w1, 0.0) + jnp.where(lanef == i2, w2, 0.0)) / (w1 + w2)
                             + jnp.where(lane == N_EXPERTS + gsel, 1.0, 0.0))
        yield

    _alternate(rows_of(0), rows_of(1))


def _merge_call(x, h, branches, mod, wg, bg, wbr, wo, l1g, l1b, rw, rb, tm=512):
    bsz, t_len, _ = x.shape
    tok = lambda w: pl.BlockSpec((1, tm, w), lambda b, t: (b, t, 0))
    const = lambda shape: pl.BlockSpec(shape, lambda b, t: (0,) * len(shape), pipeline_mode=pl.Buffered(1))
    return pl.pallas_call(
        _merge_kernel,
        grid=(bsz, t_len // tm),
        in_specs=[tok(D_MODEL), tok(D_MODEL)] + [tok(BRANCH_W)] * 4
        + [pl.BlockSpec((1, N_ADA, D_MODEL), lambda b, t: (b, 0, 0)),
           const((4, D_MODEL, D_MODEL)), const((4, D_MODEL)), const((4, BRANCH_W, D_MODEL)), const((D_MODEL, D_MODEL)),
           const((1, D_MODEL)), const((1, D_MODEL)), const((D_MODEL, LANES)), const((1, LANES))],
        out_specs=[tok(D_MODEL), tok(D_MODEL), tok(LANES)],
        out_shape=[jax.ShapeDtypeStruct((bsz, t_len, D_MODEL), F32),
                   jax.ShapeDtypeStruct((bsz, t_len, D_MODEL), BF16),
                   jax.ShapeDtypeStruct((bsz, t_len, LANES), F32)],
        compiler_params=_cparams(("parallel", "parallel")),
        name="merge",
    )(x, h, *branches, mod, wg, bg, wbr, wo, l1g, l1b, rw, rb)


def _moe_kernel(h_ref, cmb_ref, x_ref, mod_ref, wg_ref, wu_ref, wd_ref, l2g_ref, l2b_ref, o_ref,
                xs_s, cs_s, ys_s, pos_s, seg_s, *, tm):
    e = pl.program_id(2)
    nrow = tm + N_GROUPS * MOE_SEG
    lane = lax.broadcasted_iota(jnp.int32, (1, LANES), 1)

    @pl.when(e == 0)
    def _():
        cmb = cmb_ref[0]
        gh = jnp.where((lane >= N_EXPERTS) & (lane < N_EXPERTS + N_GROUPS), cmb, 0.0)
        tr = lax.broadcasted_iota(jnp.int32, (tm, tm), 0)
        tc = lax.broadcasted_iota(jnp.int32, (tm, tm), 1)
        rank = _dot(jnp.where(tc < tr, 1.0, 0.0).astype(BF16), gh.astype(BF16))
        counts = jnp.sum(gh, axis=0, keepdims=True)
        padded = jnp.floor((counts + (MOE_SEG - 1)) * (1.0 / MOE_SEG)) * MOE_SEG
        lr = lax.broadcasted_iota(jnp.int32, (LANES, LANES), 0)
        lc = lax.broadcasted_iota(jnp.int32, (LANES, LANES), 1)
        before = jnp.where(lr < lc, 1.0, 0.0).astype(BF16)
        start = _dot_split_lhs(jnp.broadcast_to(padded, (8, LANES)), before, 3)[0:1, :]
        pos = jnp.sum(gh * (start + rank), axis=-1, keepdims=True)
        posb = jnp.broadcast_to(pos, (tm, LANES))
        pos_s[...] = posb
        pos_row = posb.T[0:1, :]
        perm = jnp.where(lax.broadcasted_iota(jnp.int32, (nrow, tm), 0).astype(F32) == pos_row, 1.0, 0.0)
        perm = perm.astype(BF16)
        xs_s[...] = _dot(perm, h_ref[0]).astype(BF16)
        cs_s[...] = _dot_split_rhs(perm, cmb, 3)
        ys_s[...] = jnp.zeros_like(ys_s)
        for g in range(N_GROUPS):
            pick = lane == N_EXPERTS + g
            seg_s[g] = jnp.sum(jnp.where(pick, start, 0.0)).astype(jnp.int32)
            seg_s[N_GROUPS + g] = jnp.sum(jnp.where(pick, padded, 0.0)).astype(jnp.int32) // MOE_SEG

    grp = e // (N_EXPERTS // N_GROUPS)
    seg_start = seg_s[grp]

    def block(i, carry):
        rows = pl.ds(pl.multiple_of(seg_start + i * MOE_SEG, MOE_SEG), MOE_SEG)
        xb = xs_s[rows, :]
        a = _dot(xb, wg_ref[0])
        u = _dot(xb, wu_ref[0])
        he = (a * _sigmoid(a) * u).astype(BF16)
        ce = jnp.sum(jnp.where(lane == e, cs_s[rows, :], 0.0), axis=-1, keepdims=True)
        ys_s[rows, :] += ce * _dot(he, wd_ref[0])
        return carry

    lax.fori_loop(0, seg_s[N_GROUPS + grp], block, 0)

    @pl.when(e == N_EXPERTS - 1)
    def _():
        back = jnp.where(lax.broadcasted_iota(jnp.int32, (tm, nrow), 1).astype(F32) == pos_s[:, 0:1], 1.0, 0.0)
        y_hi, y_lo = _split_bf16(ys_s[...], 2)
        back = back.astype(BF16)
        y = _dot(back, y_hi) + _dot(back, y_lo)
        gt2 = mod_ref[0, 5:6, :]
        o_ref[0] = _ln(DN_ALPHA * x_ref[0] + gt2 * y) * l2g_ref[...] + l2b_ref[...]


def _moe_call(h2, cmb, x1, mod, wg, wu, wd, l2g, l2b, tm=1024):
    bsz, t_len, _ = x1.shape
    tok = lambda w: pl.BlockSpec((1, tm, w), lambda b, t, e: (b, t, 0))
    nrow = tm + N_GROUPS * MOE_SEG
    return pl.pallas_call(
        functools.partial(_moe_kernel, tm=tm),
        grid=(bsz, t_len // tm, N_EXPERTS),
        in_specs=[tok(D_MODEL), tok(LANES), tok(D_MODEL),
                  pl.BlockSpec((1, N_ADA, D_MODEL), lambda b, t, e: (b, 0, 0)),
                  pl.BlockSpec((1, D_MODEL, D_FF_EXPERT), lambda b, t, e: (e, 0, 0)),
                  pl.BlockSpec((1, D_MODEL, D_FF_EXPERT), lambda b, t, e: (e, 0, 0)),
                  pl.BlockSpec((1, D_FF_EXPERT, D_MODEL), lambda b, t, e: (e, 0, 0)),
                  pl.BlockSpec((1, D_MODEL), lambda b, t, e: (0, 0)),
                  pl.BlockSpec((1, D_MODEL), lambda b, t, e: (0, 0))],
        out_specs=tok(D_MODEL),
        out_shape=jax.ShapeDtypeStruct((bsz, t_len, D_MODEL), F32),
        scratch_shapes=[pltpu.VMEM((nrow, D_MODEL), BF16), pltpu.VMEM((nrow, LANES), F32),
                        pltpu.VMEM((nrow, D_MODEL), F32), pltpu.VMEM((tm, LANES), F32),
                        pltpu.SMEM((2 * N_GROUPS,), jnp.int32)],
        compiler_params=_cparams(("parallel", "parallel", "arbitrary")),
        name="moe",
    )(h2, cmb, x1, mod, wg, wu, wd, l2g, l2b)


def _pad_cols(w, n):
    return jnp.pad(w, ((0, 0), (0, n - w.shape[1])))


def kernel(x, c, ada_w, ada_b, w_in, gla_alpha_up, gla_alpha_b, gla_norm_g, rwkv_mu, rwkv_w0, rwkv_w2, rwkv_a0, rwkv_a2, rwkv_g2, rwkv_k_k, rwkv_k_a, rwkv_r_k, rwkv_ln_g, rwkv_ln_b, fox_f_bias, hgrn_lb_logits, hgrn_norm_g, w_br, w_gate, b_gate, w_o, ln1_g, ln1_b, router_w, router_b, exp_w_gate, exp_w_up, exp_w_down, ln2_g, ln2_b):
    mod_all = _ada_call(c, ada_w, ada_b)
    rw = _pad_cols(router_w, LANES)
    rb = _pad_cols(router_b.reshape(1, N_EXPERTS), LANES)
    o1, o2, o3 = GLA_COLS, GLA_COLS + RWKV_COLS, GLA_COLS + RWKV_COLS + FOX_COLS
    for i in range(DEPTH):
        mod = mod_all[i]
        wi = w_in[i]
        wa = _pad_cols(wi[:, :o1], PAD_COLS).astype(BF16)
        wb = wi[:, o1:o2].astype(BF16)
        wc = _pad_cols(wi[:, o2:o3], PAD_COLS).astype(BF16)
        wd = wi[:, o3:].astype(BF16)
        h, pa, pb, pcq, pckv, pcf, pd = _inproj_call(x, mod, wa, wb, wc, wd)
        br_a = _gla_call(pa, gla_alpha_up[i], gla_alpha_b[i], gla_norm_g[i])
        br_b = _rwkv_call(pb, rwkv_mu[i], rwkv_w0[i], rwkv_w2[i], rwkv_a0[i], rwkv_a2[i], rwkv_g2[i],
                          rwkv_k_k[i], rwkv_k_a[i], rwkv_r_k[i], rwkv_ln_g[i], rwkv_ln_b[i])
        br_c = _fox_call(pcq, pckv, _foxf_call(pcf, fox_f_bias[i]))
        br_d = _hgrn_call(pd, hgrn_lb_logits, hgrn_norm_g[i], i)
        x1, h2, cmb = _merge_call(
            x, h, (br_a, br_b, br_c, br_d), mod, w_gate[i].astype(BF16), b_gate[i], w_br[i].astype(BF16),
            w_o[i].astype(BF16), ln1_g[i].reshape(1, -1), ln1_b[i].reshape(1, -1), rw, rb)
        x = _moe_call(h2, cmb, x1, mod, exp_w_gate[i].astype(BF16), exp_w_up[i].astype(BF16),
                      exp_w_down[i].astype(BF16), ln2_g[i].reshape(1, -1), ln2_b[i].reshape(1, -1))
    return x
```

```python
import functools

import jax
import jax.numpy as jnp
from jax import lax
from jax.experimental import pallas as pl
from jax.experimental.pallas import tpu as pltpu

F32 = jnp.float32
BF16 = jnp.bfloat16
HI = lax.Precision.HIGHEST

D_MODEL = 1024
DEPTH = 2
N_ADA = 6
BRANCH_W = 512
GLA_HEADS = 4
GLA_DK = 64
GLA_KW = 256
GLA_GATE_RANK = 16
GLA_LOGIT_NORM = 16.0
RWKV_HEAD = 64
RWKV_COLS = 1792
RWKV_GN_EPS = 64e-5
FOX_HEADS = 8
FOX_DH = 64
HGRN_KW = 512
N_EXPERTS = 16
N_GROUPS = 4
D_FF_EXPERT = 512
MOE_SEG = 128
DN_ALPHA = (2.0 * DEPTH) ** 0.25
GLA_COLS = 1552
FOX_COLS = 1544
HGRN_COLS = 2048
PAD_COLS = 1664

LANES = 128
CHUNK = 64
SUB = 16
FACTORED_MAX_DECAY = 80.0
FACTORED_MAX_ABS = 1e3
VMEM_LIMIT = 56 * 1024 * 1024


def _cparams(sem):
    return pltpu.CompilerParams(dimension_semantics=sem, vmem_limit_bytes=VMEM_LIMIT)


def _sigmoid(x):
    return 1.0 / (1.0 + jnp.exp(-x))


def _log_sigmoid(x):
    return jnp.minimum(x, 0.0) - jnp.log(1.0 + jnp.exp(-jnp.abs(x)))


def _ln(x, eps=1e-5):
    mu = jnp.mean(x, axis=-1, keepdims=True)
    xc = x - mu
    var = jnp.mean(xc * xc, axis=-1, keepdims=True)
    return xc * lax.rsqrt(var + eps)


def _tri(n, strict=False):
    r = lax.broadcasted_iota(jnp.int32, (n, n), 0)
    c = lax.broadcasted_iota(jnp.int32, (n, n), 1)
    return jnp.where((c < r) if strict else (c <= r), 1.0, 0.0).astype(F32)


def _dot(a, b, prec=None):
    return jnp.dot(a, b, preferred_element_type=F32, precision=prec)


def _dot_nt(a, b, prec=None):
    return lax.dot_general(a, b, (((1,), (1,)), ((), ())), preferred_element_type=F32, precision=prec)


def _alternate(*gens):
    live = list(gens)
    while live:
        for gen in list(live):
            if next(gen, "done") == "done":
                live.remove(gen)


def _dot_tn(a, b, prec=None):
    return lax.dot_general(a, b, (((0,), (0,)), ((), ())), preferred_element_type=F32, precision=prec)


def _ada_kernel(c_ref, w_ref, b_ref, o_ref):
    c = c_ref[...]
    cond = c * _sigmoid(c)
    o_ref[0] = _dot(cond, w_ref[0], HI) + b_ref[0]


def _ada_call(c, ada_w, ada_b):
    bsz = c.shape[0]
    n = N_ADA * D_MODEL
    bn = 1536
    out = pl.pallas_call(
        _ada_kernel,
        grid=(DEPTH, n // bn),
        in_specs=[
            pl.BlockSpec((bsz, D_MODEL), lambda i, j: (0, 0)),
            pl.BlockSpec((1, D_MODEL, bn), lambda i, j: (i, 0, j)),
            pl.BlockSpec((1, 1, bn), lambda i, j: (i, 0, j)),
        ],
        out_specs=pl.BlockSpec((1, bsz, bn), lambda i, j: (i, 0, j)),
        out_shape=jax.ShapeDtypeStruct((DEPTH, bsz, n), F32),
        compiler_params=_cparams(("parallel", "parallel")),
        name="ada_mod",
    )(c, ada_w, ada_b.reshape(DEPTH, 1, n))
    return out.reshape(DEPTH, bsz, N_ADA, D_MODEL)


def _inproj_kernel(x_ref, mod_ref, wa_ref, wb_ref, wc_ref, wd_ref,
                   h_ref, oa_ref, ob_ref, ocq_ref, ockv_ref, ocf_ref, od_ref):
    x = x_ref[0]
    sh = mod_ref[0, 0:1, :]
    sc = mod_ref[0, 1:2, :]
    hb = (_ln(x) * (1.0 + sc) + sh).astype(BF16)
    h_ref[0] = hb
    oa_ref[0] = _dot(hb, wa_ref[...])
    ob_ref[0] = _dot(hb, wb_ref[...])
    oc = _dot(hb, wc_ref[...])
    w = BRANCH_W
    ocq_ref[0] = oc[:, 0:w]
    ockv_ref[0] = oc[:, w:3 * w].astype(BF16)
    ocf_ref[0] = oc[:, 3 * w:3 * w + LANES]
    od_ref[0] = _dot(hb, wd_ref[...])


def _inproj_call(x, mod, wa, wb, wc, wd, tm=256):
    bsz, t_len, _ = x.shape
    widths = (wa.shape[1], wb.shape[1], wc.shape[1], wd.shape[1])
    outs = [(D_MODEL, BF16), (widths[0], F32), (widths[1], F32), (BRANCH_W, F32), (2 * BRANCH_W, BF16),
            (LANES, F32), (widths[3], F32)]

    def wspec(w):
        return pl.BlockSpec((D_MODEL, w), lambda b, t: (0, 0), pipeline_mode=pl.Buffered(1))

    return pl.pallas_call(
        _inproj_kernel,
        grid=(bsz, t_len // tm),
        in_specs=[pl.BlockSpec((1, tm, D_MODEL), lambda b, t: (b, t, 0)),
                  pl.BlockSpec((1, N_ADA, D_MODEL), lambda b, t: (b, 0, 0))] + [wspec(w) for w in widths],
        out_specs=[pl.BlockSpec((1, tm, w), lambda b, t: (b, t, 0)) for w, _ in outs],
        out_shape=[jax.ShapeDtypeStruct((bsz, t_len, w), dt) for w, dt in outs],
        compiler_params=_cparams(("parallel", "parallel")),
        name="inproj",
    )(x, mod, wa, wb, wc, wd)


def _gla_kernel(*refs, mode, layer, hp, qscale, tm, ngrp):
    if mode == "gla":
        q_ref, k_ref, v_ref, gate_ref, al_ref, aup_ref, ab_ref, ng_ref, sel_ref, o_ref = refs[:10]
    else:
        q_ref, k_ref, v_ref, gate_ref, lbl_ref, ng_ref, sel_ref, o_ref = refs[:8]
    st_ref, q_s, k_s, b_s, qe_s, kd_s, pw_s = refs[-7:]
    c = CHUNK
    nb = c // SUB
    dk = LANES // hp

    @pl.when(pl.program_id(1) == 0)
    def _():
        st_ref[...] = jnp.zeros_like(st_ref)

    q = q_ref[0]
    if qscale != 1.0:
        q = q * qscale
    kin = k_ref[0]
    if mode == "gla":
        g = _log_sigmoid(_dot(al_ref[0], aup_ref[...], HI) + ab_ref[...]) / GLA_LOGIT_NORM
        k = kin
    else:
        lg = lbl_ref[...]
        mx = jnp.max(lg, axis=0, keepdims=True)
        ex = jnp.exp(lg - mx)
        p = ex / jnp.sum(ex, axis=0, keepdims=True)
        cum = p[0:1, :]
        for j in range(1, layer + 1):
            cum = cum + p[j:j + 1, :]
        lb = cum - p[0:1, :]
        g = jnp.log(lb + (1.0 - lb) * _sigmoid(kin))
        k = (1.0 - lb) * _sigmoid(-kin)
    tr = lax.broadcasted_iota(jnp.int32, (tm, tm), 0)
    tc = lax.broadcasted_iota(jnp.int32, (tm, tm), 1)
    same = (tr // c) == (tc // c)
    b = _dot_split_rhs(jnp.where((tc <= tr) & same, 1.0, 0.0).astype(BF16), g, 3)
    nch = tm // c
    b_end = jnp.concatenate([jnp.broadcast_to(b[(ci + 1) * c - 1:(ci + 1) * c, :], (c, b.shape[1]))
                             for ci in range(nch)], axis=0)
    qe = (q * jnp.exp(b)).astype(BF16)
    kd = (k * jnp.exp(b_end - b)).astype(BF16)

    lane = lax.broadcasted_iota(jnp.int32, (1, LANES), 1)
    hmask = [((lane >= h * dk) & (lane < (h + 1) * dk)) for h in range(hp)]
    row = lax.broadcasted_iota(jnp.int32, (c, c), 0)
    col = lax.broadcasted_iota(jnp.int32, (c, c), 1)
    off_mask = (col // SUB) < (row // SUB)
    diag_mask = ((col // SUB) == (row // SUB)) & (col <= row)
    ng = ng_ref[...]

    def head_lanes(x, h):
        return x if hp == 1 else jnp.where(hmask[h], x, jnp.zeros_like(x))

    def finish(rows, hv, o):
        o = o * lax.rsqrt(jnp.mean(o * o, axis=-1, keepdims=True) + 1e-6) * ng
        gt = gate_ref[0, rows, hv * LANES:(hv + 1) * LANES]
        o_ref[0, rows, hv * LANES:(hv + 1) * LANES] = o * (gt * _sigmoid(gt))

    def intra_exact(rows):
        qhs, khats = {}, []
        for gi in range(ngrp):
            gl = slice(gi * LANES, (gi + 1) * LANES)
            qc = q_s[rows, gl]
            kc = k_s[rows, gl]
            bc = b_s[rows, gl]
            rblk = jnp.concatenate([jnp.broadcast_to(bc[m * SUB:m * SUB + 1, :], (SUB, LANES)) for m in range(nb)],
                                   axis=0)
            qh = (qc * jnp.exp(bc - rblk)).astype(BF16)
            for h in range(hp):
                qhs[gi * hp + h] = head_lanes(qh, h)
            khats.append([None] + [(kc * jnp.exp(jnp.minimum(bc[m * SUB:m * SUB + 1, :] - bc, 0.0))).astype(BF16)
                                   for m in range(1, nb)])
            for jj in range(SUB):
                kb = jnp.concatenate([jnp.broadcast_to(kc[m * SUB + jj:m * SUB + jj + 1, :], (SUB, LANES))
                                      for m in range(nb)], axis=0)
                bb = jnp.concatenate([jnp.broadcast_to(bc[m * SUB + jj:m * SUB + jj + 1, :], (SUB, LANES))
                                      for m in range(nb)], axis=0)
                pw_s[gi * c:(gi + 1) * c, jj * LANES:(jj + 1) * LANES] = (
                    qc * kb * jnp.exp(jnp.minimum(bc - bb, 0.0))).astype(BF16)
        a_offs = {}
        for hv in range(ngrp * hp):
            a_offs[hv] = jnp.concatenate(
                [jnp.zeros((SUB, c), F32)]
                + [_dot_nt(qhs[hv][m * SUB:(m + 1) * SUB, :], khats[hv // hp][m]) for m in range(1, nb)], axis=0)
        a_diags = [_dot(pw_s[...], sel_ref[h]) for h in range(hp)]
        out = {}
        for hv in range(ngrp * hp):
            gi, h = hv // hp, hv % hp
            out[hv] = jnp.where(off_mask, a_offs[hv], jnp.where(diag_mask, a_diags[h][gi * c:(gi + 1) * c], 0.0))
        return out

    def chunk(ci, carry):
        r0 = pl.multiple_of(ci * c, c)
        rows = pl.ds(r0, c)
        sts, o_int, upds, vbs = [], {}, {}, {}
        for gi in range(ngrp):
            gl = slice(gi * LANES, (gi + 1) * LANES)
            sts.append(st_ref[gi])
            stb = sts[gi].astype(BF16)
            for h in range(hp):
                hv = gi * hp + h
                vbs[hv] = v_ref[0, rows, hv * LANES:(hv + 1) * LANES].astype(BF16)
                o_int[hv] = _dot_nt(head_lanes(qe_s[rows, gl], h), stb)
                upds[hv] = _dot_tn(vbs[hv], kd_s[rows, gl])
        attn = intra_exact(rows)
        for hv in range(ngrp * hp):
            finish(rows, hv, o_int[hv] + _dot(attn[hv].astype(BF16), vbs[hv]))
        for gi in range(ngrp):
            gl = slice(gi * LANES, (gi + 1) * LANES)
            b_last = b_s[pl.ds(pl.multiple_of(r0 + c - 8, 8), 8), gl][7:8, :]
            new_st = sts[gi] * jnp.exp(b_last)
            for h in range(hp):
                new_st = new_st + head_lanes(upds[gi * hp + h], h)
            st_ref[gi] = new_st
        return carry

    small_decay = jnp.max(jnp.maximum(-b * (1.0 / FACTORED_MAX_DECAY),
                                      jnp.maximum(jnp.abs(q), jnp.abs(k)) * (1.0 / FACTORED_MAX_ABS))) < 1.0

    @pl.when(small_decay)
    def _():
        heads = [(ci, hv) for ci in range(nch) for hv in range(ngrp * hp)]
        b0 = jnp.concatenate([jnp.broadcast_to(b[ci * c:ci * c + 1, :], (c, b.shape[1])) for ci in range(nch)], axis=0)
        bz = b - b0
        qz = (q * jnp.exp(bz)).astype(BF16)
        kz = (k * jnp.exp(-bz)).astype(BF16)
        sl = lambda ci, gi: (slice(ci * c, (ci + 1) * c), slice(gi * LANES, (gi + 1) * LANES))
        vbs, upds, attn, o_intra, o_int = {}, {}, {}, {}, {}
        for ci, hv in heads:
            vbs[ci, hv] = v_ref[0, ci * c:(ci + 1) * c, hv * LANES:(hv + 1) * LANES].astype(BF16)
            upds[ci, hv] = _dot_tn(vbs[ci, hv], kd[sl(ci, hv // hp)])
        for ci, hv in heads:
            attn[ci, hv] = jnp.where(col <= row, _dot_nt(head_lanes(qz[sl(ci, hv // hp)], hv % hp),
                                                           kz[sl(ci, hv // hp)]), 0.0).astype(BF16)
        for ci, hv in heads:
            o_intra[ci, hv] = _dot(attn[ci, hv], vbs[ci, hv])
        sts = [[st_ref[gi]] for gi in range(ngrp)]
        for ci in range(nch):
            for gi in range(ngrp):
                new_st = sts[gi][ci] * jnp.exp(b[(ci + 1) * c - 1:(ci + 1) * c, gi * LANES:(gi + 1) * LANES])
                for h in range(hp):
                    new_st = new_st + head_lanes(upds[ci, gi * hp + h], h)
                sts[gi].append(new_st)
        for gi in range(ngrp):
            st_ref[gi] = sts[gi][nch]
        for ci, hv in heads:
            o_int[ci, hv] = _dot_nt(head_lanes(qe[sl(ci, hv // hp)], hv % hp), sts[hv // hp][ci].astype(BF16))
        for ci, hv in heads:
            finish(slice(ci * c, (ci + 1) * c), hv, o_int[ci, hv] + o_intra[ci, hv])

    @pl.when(jnp.logical_not(small_decay))
    def _():
        q_s[...] = q
        k_s[...] = k
        b_s[...] = b
        qe_s[...] = qe
        kd_s[...] = kd
        lax.fori_loop(0, nch, chunk, 0)


def _pair_select(hp):
    dk = LANES // hp
    r = jnp.arange(SUB * LANES)
    jj = r // LANES
    head = (r % LANES) // dk
    cj = jnp.arange(CHUNK) % SUB
    sel = (jj[None, :, None] == cj[None, None, :]) & (head[None, :, None] == jnp.arange(hp)[:, None, None])
    return sel.astype(BF16)


def _gla_scratch(tm, ngrp):
    w = ngrp * LANES
    return ([pltpu.VMEM((ngrp, LANES, LANES), F32)] + [pltpu.VMEM((tm, w), F32) for _ in range(3)]
            + [pltpu.VMEM((tm, w), BF16) for _ in range(2)] + [pltpu.VMEM((ngrp * CHUNK, SUB * LANES), BF16)])


def _gla_call(proj, alpha_up, alpha_b, norm_g, tm=256):
    bsz, t_len, _ = proj.shape
    hp = 2
    ngrp = GLA_HEADS // hp
    aup = jnp.zeros((LANES, GLA_KW), F32).at[:GLA_GATE_RANK].set(alpha_up)
    blk = lambda w, j: pl.BlockSpec((1, tm, w), lambda b, t: (b, t, j))
    const = lambda shape: pl.BlockSpec(shape, lambda b, t: (0,) * len(shape))
    kern = functools.partial(_gla_kernel, mode="gla", layer=0, hp=hp, qscale=GLA_DK ** -0.5, tm=tm, ngrp=ngrp)
    return pl.pallas_call(
        kern,
        grid=(bsz, t_len // tm),
        in_specs=[blk(GLA_KW, 0), blk(GLA_KW, 1), blk(BRANCH_W, 1), blk(BRANCH_W, 2), blk(LANES, 12),
                  const((LANES, GLA_KW)), const((1, GLA_KW)), const((1, LANES)), const((hp, SUB * LANES, CHUNK))],
        out_specs=blk(BRANCH_W, 0),
        out_shape=jax.ShapeDtypeStruct((bsz, t_len, BRANCH_W), F32),
        scratch_shapes=_gla_scratch(tm, ngrp),
        compiler_params=_cparams(("parallel", "arbitrary")),
        name="gla",
    )(proj, proj, proj, proj, proj, aup, alpha_b.reshape(1, GLA_KW), norm_g.reshape(1, LANES), _pair_select(hp))


def _hgrn_call(proj, lb_logits, norm_g, layer, tm=256):
    bsz, t_len, _ = proj.shape
    ngrp = HGRN_KW // LANES
    blk = lambda j: pl.BlockSpec((1, tm, HGRN_KW), lambda b, t: (b, t, j))
    const = lambda shape: pl.BlockSpec(shape, lambda b, t: (0,) * len(shape))
    kern = functools.partial(_gla_kernel, mode="hgrn", layer=layer, hp=1, qscale=1.0, tm=tm, ngrp=ngrp)
    return pl.pallas_call(
        kern,
        grid=(bsz, t_len // tm),
        in_specs=[blk(0), blk(1), blk(2), blk(3), const((DEPTH, HGRN_KW)), const((1, LANES)),
                  const((1, SUB * LANES, CHUNK))],
        out_specs=blk(0),
        out_shape=jax.ShapeDtypeStruct((bsz, t_len, BRANCH_W), F32),
        scratch_shapes=_gla_scratch(tm, ngrp),
        compiler_params=_cparams(("parallel", "arbitrary")),
        name="hgrn",
    )(proj, proj, proj, proj, lb_logits, norm_g.reshape(1, LANES), _pair_select(1))


def _split_bf16(x, n):
    parts, rest = [], x
    for i in range(n):
        part = rest.astype(BF16)
        parts.append(part)
        if i + 1 < n:
            rest = rest - part.astype(F32)
    return parts


def _dot_split_rhs(a01, x, n):
    out = None
    for part in _split_bf16(x, n):
        term = _dot(a01, part)
        out = term if out is None else out + term
    return out


def _dot_split_lhs(x, b01, n):
    out = None
    for part in _split_bf16(x, n):
        term = _dot(part, b01)
        out = term if out is None else out + term
    return out


def _rwkv_kernel(cols_ref, mu_ref, w0_ref, w2_ref, a0_ref, a2_ref, g2_ref, kk_ref, ka_ref, rk_ref, lng_ref, lnb_ref,
                 o_ref, carry_ref, st_ref, rt_s, at_s, bt_s, kt_s, vb_s, w2_s, eg_s, y_s, w1_s, tops_s, *, tm):
    w = BRANCH_W
    c = CHUNK
    npair = w // LANES

    @pl.when(pl.program_id(1) == 0)
    def _():
        carry_ref[...] = jnp.zeros_like(carry_ref)
        st_ref[...] = jnp.zeros_like(st_ref)

    cols = cols_ref[0]
    rowi = lax.broadcasted_iota(jnp.int32, cols.shape, 0)
    prev = jnp.where(rowi == 0, carry_ref[0:1, :], pltpu.roll(cols, 1, 0))
    carry_ref[0:1, :] = cols[tm - 1:tm, :]
    xs = cols + (prev - cols) * mu_ref[...]
    r = xs[:, 0:w]
    k = xs[:, w:2 * w]
    v = xs[:, 2 * w:3 * w]
    wl = xs[:, 3 * w:3 * w + LANES]
    gl = xs[:, 3 * w + LANES:3 * w + 2 * LANES]
    w_raw = _log_sigmoid(w0_ref[...] + _dot(jnp.tanh(wl).astype(BF16), w2_ref[...])) - 0.5
    lw = -jnp.exp(w_raw)
    a = _sigmoid(a0_ref[...] + _dot(wl.astype(BF16), a2_ref[...]))
    g = _dot(_sigmoid(gl).astype(BF16), g2_ref[...])

    hr = lax.broadcasted_iota(jnp.int32, (LANES, LANES), 0) // RWKV_HEAD
    hc = lax.broadcasted_iota(jnp.int32, (LANES, LANES), 1) // RWKV_HEAD
    bdmask = hr == hc
    bd = jnp.where(bdmask, 1.0, 0.0).astype(BF16)

    def head_sum(t, n):
        return jnp.concatenate([_dot_split_lhs(t[:, p * LANES:(p + 1) * LANES], bd, n) for p in range(npair)], axis=1)

    kk = k * kk_ref[...]
    kk = kk / jnp.maximum(jnp.sqrt(head_sum(kk * kk, 3)), 1e-12)
    k = k * (1.0 + (a - 1.0) * ka_ref[...])

    tr = lax.broadcasted_iota(jnp.int32, (tm, tm), 0)
    tc = lax.broadcasted_iota(jnp.int32, (tm, tm), 1)
    tri = jnp.where((tc <= tr) & ((tr // c) == (tc // c)), 1.0, 0.0).astype(BF16)
    gam = _dot_split_rhs(tri, lw, 3)
    eg = jnp.exp(gam)
    ieg = jnp.exp(-gam)
    rt_s[...] = (r * eg).astype(BF16)
    at_s[...] = (-kk * jnp.exp(gam - lw)).astype(BF16)
    bt_s[...] = (kk * a * ieg).astype(BF16)
    kt_s[...] = (k * ieg).astype(BF16)
    vb_s[...] = v.astype(BF16)
    eg_s[...] = eg

    lane = lax.broadcasted_iota(jnp.int32, (1, LANES), 1)
    lo = lane < RWKV_HEAD
    hmask = [lo, jnp.logical_not(lo)]
    row = lax.broadcasted_iota(jnp.int32, (c, LANES), 0)
    col = lax.broadcasted_iota(jnp.int32, (c, LANES), 1)
    top_mask = (col % c) <= row
    ak_mask = (col >= c) & ((col - c) < row)
    r64 = lax.broadcasted_iota(jnp.int32, (c, c), 0)
    c64 = lax.broadcasted_iota(jnp.int32, (c, c), 1)
    strict = c64 < r64
    eye = jnp.where(c64 == r64, 1.0, 0.0).astype(F32)

    def solve(j):
        chains = []
        for cc in range(2):
            r0 = (2 * j + cc) * c
            for p in range(npair):
                ls = slice(p * LANES, (p + 1) * LANES)
                at = at_s[pl.ds(r0, c), ls]
                vb = vb_s[pl.ds(r0, c), ls]
                ra = jnp.concatenate([rt_s[pl.ds(r0, c), ls], at], axis=0)
                bk = jnp.concatenate([bt_s[pl.ds(r0, c), ls], kt_s[pl.ds(r0, c), ls]], axis=0)
                vv = jnp.concatenate([vb, vb], axis=0)
                for h in range(2):
                    x = _dot_nt(jnp.where(hmask[h], ra, jnp.zeros_like(ra)), bk)
                    tops_s[pl.ds(r0, c), (2 * p + h) * LANES:(2 * p + h + 1) * LANES] = (
                        jnp.where(top_mask, x[0:c], 0.0).astype(BF16))
                    chains.append(dict(r0=r0, p=p, h=h, vv=vv, atm=jnp.where(hmask[h], at, jnp.zeros_like(at)),
                                       aab=jnp.where(strict, x[c:2 * c, 0:c], 0.0),
                                       akm=jnp.where(ak_mask, x[c:2 * c], 0.0).astype(BF16)))
            yield
        for ch in chains:
            ab = ch["aab"].astype(BF16)
            ch["xp"] = _dot(ab, ab)
            ch["t"] = eye + ch["aab"]
        yield
        for ch in chains:
            ch["akv"] = _dot(ch["akm"], ch["vv"])
        yield
        for stage in range(5):
            for ch in chains:
                xb = ch["xp"].astype(BF16)
                if stage < 4:
                    res = _dot(jnp.concatenate([ch["t"].astype(BF16), xb], axis=0), xb)
                    ch["t"] = ch["t"] + res[0:c]
                    ch["xp"] = res[c:2 * c]
                else:
                    ch["t"] = ch["t"] + _dot(ch["t"].astype(BF16), xb)
            yield
        for ch in chains:
            z = jnp.concatenate([ch["akv"].astype(BF16), ch["atm"]], axis=1)
            ch["tz"] = _dot(ch["t"].astype(BF16), z)
        for i in range(0, len(chains), 2):
            c0, c1 = chains[i], chains[i + 1]
            ls = slice(c0["p"] * LANES, (c0["p"] + 1) * LANES)
            w1_s[pl.ds(c0["r0"], c), ls] = jnp.where(lo, c0["tz"][:, 0:LANES], c1["tz"][:, 0:LANES])
            w2_s[pl.ds(c0["r0"], c), ls] = (c0["tz"][:, LANES:2 * LANES] + c1["tz"][:, LANES:2 * LANES]).astype(BF16)
        yield

    def scan(ci_):
        r0 = ci_ * c
        pts, rps, bks, vbs = [], [], [], []
        for p in range(npair):
            ls = slice(p * LANES, (p + 1) * LANES)
            pt = st_ref[p]
            rw = jnp.concatenate([rt_s[pl.ds(r0, c), ls], w2_s[pl.ds(r0, c), ls]], axis=0)
            pts.append(pt)
            rps.append(_dot_nt(rw, pt.astype(BF16)))
        yield
        uvs = []
        for p in range(npair):
            ls = slice(p * LANES, (p + 1) * LANES)
            u = w1_s[pl.ds(r0, c), ls] + rps[p][c:2 * c]
            vb = vb_s[pl.ds(r0, c), ls]
            uvs.append(jnp.concatenate([u.astype(BF16), vb], axis=0))
            bks.append(jnp.concatenate([bt_s[pl.ds(r0, c), ls], kt_s[pl.ds(r0, c), ls]], axis=0))
        for p in range(npair):
            ls = slice(p * LANES, (p + 1) * LANES)
            upd = _dot_tn(uvs[p], bks[p])
            eg_last = eg_s[pl.ds(r0 + c - 1, 1), ls]
            st_ref[p] = jnp.where(bdmask, (pts[p] + upd) * eg_last, 0.0)
        yield
        for p in range(npair):
            ls = slice(p * LANES, (p + 1) * LANES)
            y0 = _dot(tops_s[pl.ds(r0, c), (2 * p) * LANES:(2 * p + 1) * LANES], uvs[p])
            y1 = _dot(tops_s[pl.ds(r0, c), (2 * p + 1) * LANES:(2 * p + 2) * LANES], uvs[p])
            y_s[pl.ds(r0, c), ls] = rps[p][0:c] + jnp.where(lo, y0, y1)
        yield

    def scan_pair(j):
        yield from scan(2 * j)
        yield from scan(2 * j + 1)

    npairs_t = tm // (2 * c)
    _alternate(solve(0))
    for j in range(1, npairs_t):
        _alternate(solve(j), scan_pair(j - 1))
    _alternate(scan_pair(npairs_t - 1))

    y = y_s[...]
    inv = 1.0 / RWKV_HEAD
    mean = head_sum(y, 2) * inv
    yc = y - mean
    var = head_sum(yc * yc, 2) * inv
    yn = yc * lax.rsqrt(var + RWKV_GN_EPS) * lng_ref[...] + lnb_ref[...]
    bonus = head_sum(r * k * rk_ref[...], 2) * v
    o_ref[0] = (yn + bonus) * g


def _rwkv_call(proj, mu, w0, w2, a0, a2, g2, k_k, k_a, r_k, ln_g, ln_b, tm=512):
    bsz, t_len, _ = proj.shape
    w = BRANCH_W
    w2p = jnp.zeros((LANES, w), F32).at[:64].set(w2).astype(BF16)
    a2p = jnp.zeros((LANES, w), F32).at[64:].set(a2).astype(BF16)
    vec = lambda n: pl.BlockSpec((1, n), lambda b, t: (0, 0))
    mat = lambda n: pl.BlockSpec((n, w), lambda b, t: (0, 0))
    return pl.pallas_call(
        functools.partial(_rwkv_kernel, tm=tm),
        grid=(bsz, t_len // tm),
        in_specs=[pl.BlockSpec((1, tm, RWKV_COLS), lambda b, t: (b, t, 0)),
                  vec(RWKV_COLS), vec(w), mat(LANES), vec(w), mat(LANES), mat(LANES),
                  vec(w), vec(w), vec(w), vec(w), vec(w)],
        out_specs=pl.BlockSpec((1, tm, w), lambda b, t: (b, t, 0)),
        out_shape=jax.ShapeDtypeStruct((bsz, t_len, w), F32),
        scratch_shapes=[pltpu.VMEM((8, RWKV_COLS), F32), pltpu.VMEM((w // LANES, LANES, LANES), F32)]
        + [pltpu.VMEM((tm, w), BF16) for _ in range(6)] + [pltpu.VMEM((tm, w), F32) for _ in range(3)]
        + [pltpu.VMEM((tm, 2 * w), BF16)],
        compiler_params=_cparams(("parallel", "arbitrary")),
        name="rwkv7",
    )(proj, mu.reshape(1, -1), w0.reshape(1, w), w2p, a0.reshape(1, w), a2p, g2.astype(BF16),
      k_k.reshape(1, w), k_a.reshape(1, w), r_k.reshape(1, w), ln_g.reshape(1, w), ln_b.reshape(1, w))


LOG2E = 1.4426950408889634
FOX_AUG = 3


def _fox_aug_base(h):
    return FOX_DH * (1 - h) + 1


def _foxf_kernel(fl_ref, fb_ref, kaug_ref, carry_ref, *, tm):
    @pl.when(pl.program_id(1) == 0)
    def _():
        carry_ref[...] = jnp.zeros_like(carry_ref)

    lf = _log_sigmoid(fl_ref[0] + fb_ref[...])
    cum = _dot_split_rhs(_tri(tm).astype(BF16), lf, 3) + carry_ref[0:1, :]
    carry_ref[0:1, :] = cum[tm - 1:tm, :]
    lane = lax.broadcasted_iota(jnp.int32, (1, LANES), 1)
    for hh in range(FOX_HEADS):
        base = _fox_aug_base(hh % 2)
        terms = _split_bf16(jnp.broadcast_to(cum[:, hh:hh + 1], cum.shape) * LOG2E, FOX_AUG)
        aug = jnp.where((lane >= base + FOX_AUG) & (lane < base + 2 * FOX_AUG), 1.0, 0.0).astype(BF16)
        aug = jnp.broadcast_to(aug, cum.shape)
        for i, term in enumerate(terms):
            aug = jnp.where(lane == base + i, term, aug)
        kaug_ref[0, hh] = aug


def _foxf_call(fl, f_bias, tm=512):
    bsz, t_len, _ = fl.shape
    fb = jnp.zeros((1, LANES), F32).at[0, :FOX_HEADS].set(f_bias)
    return pl.pallas_call(
        functools.partial(_foxf_kernel, tm=tm),
        grid=(bsz, t_len // tm),
        in_specs=[pl.BlockSpec((1, tm, LANES), lambda b, t: (b, t, 0)),
                  pl.BlockSpec((1, LANES), lambda b, t: (0, 0))],
        out_specs=pl.BlockSpec((1, FOX_HEADS, tm, LANES), lambda b, t: (b, 0, t, 0)),
        out_shape=jax.ShapeDtypeStruct((bsz, FOX_HEADS, t_len, LANES), BF16),
        scratch_shapes=[pltpu.VMEM((8, LANES), F32)],
        compiler_params=_cparams(("parallel", "arbitrary")),
        name="fox_cumf",
    )(fl, fb)


def _fox_kernel(q_ref, k_ref, v_ref, kaug_ref, o_ref, m_s, acc_s, qm_s, *, tq):
    qi = pl.program_id(2)
    lane = lax.broadcasted_iota(jnp.int32, (1, LANES), 1)
    own = [lane < FOX_DH, lane >= FOX_DH]
    m_s[...] = jnp.full_like(m_s, -jnp.inf)
    acc_s[...] = jnp.zeros_like(acc_s)
    q = q_ref[0] * (FOX_DH ** -0.5 * LOG2E)
    for h in range(2):
        base = _fox_aug_base(h)
        first = kaug_ref[0, h, pl.ds(pl.multiple_of(qi * tq, tq), 8), :].astype(F32)
        f0_terms = pltpu.roll(first, FOX_AUG, 1)[0:1, :]
        q_aug = jnp.where((lane >= base) & (lane < base + FOX_AUG), -1.0,
                          jnp.where((lane >= base + FOX_AUG) & (lane < base + 2 * FOX_AUG), f0_terms, 0.0))
        qm_s[h] = jnp.where(own[h], q, q_aug).astype(BF16)
    den = [FOX_DH, 0]

    def step(ki, masked):
        rows = pl.ds(pl.multiple_of(ki * tq, tq), tq)
        kb = k_ref[0, rows, :]
        vb = v_ref[0, rows, :]
        sts = [_dot_nt(jnp.where(own[h], kb, kaug_ref[0, h, rows, :]), qm_s[h]) for h in range(2)]
        yield
        ps, alphas = [], []
        for h in range(2):
            s = sts[h]
            if masked:
                krow = lax.broadcasted_iota(jnp.int32, (tq, tq), 0)
                qcol = lax.broadcasted_iota(jnp.int32, (tq, tq), 1)
                s = jnp.where(krow <= qcol, s, -1e30)
            m_old = m_s[h]
            m_new = jnp.maximum(m_old, jnp.max(s, axis=0, keepdims=True))
            alpha = jnp.exp2(m_old - m_new)
            m_s[h] = m_new
            ps.append(jnp.exp2(s - m_new).astype(BF16))
            alphas.append(alpha)
        yield
        for h in range(2):
            v_den = jnp.where(lane == den[h], jnp.ones_like(vb), vb)
            acc_s[h] = alphas[h] * acc_s[h] + _dot_tn(v_den, ps[h])
        yield

    def earlier_pair(i, carry):
        _alternate(step(2 * i, False), step(2 * i + 1, False))
        return carry

    lax.fori_loop(0, qi // 2, earlier_pair, 0)

    @pl.when(qi % 2 == 1)
    def _():
        _alternate(step(qi - 1, False))

    _alternate(step(qi, True))
    o0 = acc_s[0] / acc_s[0][den[0]:den[0] + 1, :]
    o1 = acc_s[1] / acc_s[1][den[1]:den[1] + 1, :]
    chan = lax.broadcasted_iota(jnp.int32, (LANES, 1), 0)
    o_ref[0] = jnp.where(chan < FOX_DH, o0, o1).T


def _fox_call(q, kv, kaug, tq=512):
    bsz, t_len, _ = q.shape
    npairs = FOX_HEADS // 2
    return pl.pallas_call(
        functools.partial(_fox_kernel, tq=tq),
        grid=(bsz, npairs, t_len // tq),
        in_specs=[
            pl.BlockSpec((1, tq, LANES), lambda b, p, i: (b, i, p)),
            pl.BlockSpec((1, t_len, LANES), lambda b, p, i: (b, 0, p)),
            pl.BlockSpec((1, t_len, LANES), lambda b, p, i: (b, 0, npairs + p)),
            pl.BlockSpec((1, 2, t_len, LANES), lambda b, p, i: (b, p, 0, 0)),
        ],
        out_specs=pl.BlockSpec((1, tq, LANES), lambda b, p, i: (b, i, p)),
        out_shape=jax.ShapeDtypeStruct((bsz, t_len, BRANCH_W), F32),
        scratch_shapes=[pltpu.VMEM((2, 1, tq), F32), pltpu.VMEM((2, LANES, tq), F32),
                        pltpu.VMEM((2, tq, LANES), BF16)],
        compiler_params=_cparams(("parallel", "parallel", "arbitrary")),
        name="fox_attn",
    )(q, kv, kv, kaug)


def _merge_kernel(x_ref, h_ref, ba_ref, bb_ref, bc_ref, bd_ref, mod_ref, wg_ref, bg_ref, wbr_ref, wo_ref,
                  l1g_ref, l1b_ref, rw_ref, rb_ref, x1_ref, h2_ref, cmb_ref):
    tm = x_ref.shape[1]
    half = tm // 2

    def rows_of(r):
        rs = slice(r * half, (r + 1) * half)
        hb = h_ref[0, rs, :]
        merged = None
        for n, br_ref in enumerate((ba_ref, bb_ref, bc_ref, bd_ref)):
            gate = _sigmoid(_dot(hb, wg_ref[n]) + bg_ref[n:n + 1, :])
            term = gate * _dot(br_ref[0, rs, :].astype(BF16), wbr_ref[n])
            merged = term if merged is None else merged + term
        yield
        y = _dot(merged.astype(BF16), wo_ref[...])
        yield
        gt1 = mod_ref[0, 2:3, :]
        x1 = _ln(DN_ALPHA * x_ref[0, rs, :] + gt1 * y) * l1g_ref[...] + l1b_ref[...]
        x1_ref[0, rs, :] = x1
        h2 = _ln(x1) * (1.0 + mod_ref[0, 4:5, :]) + mod_ref[0, 3:4, :]
        h2_ref[0, rs, :] = h2.astype(BF16)

        lane = lax.broadcasted_iota(jnp.int32, (half, LANES), 1)
        lanef = lane.astype(F32)
        neg = -jnp.inf
        h_hi, h_lo = _split_bf16(h2, 2)
        r_hi, r_lo = _split_bf16(rw_ref[...], 2)
        logits = jnp.where(lane < N_EXPERTS, _dot(h_hi, r_hi) + (_dot(h_hi, r_lo) + _dot(h_lo, r_hi)), neg)
        yield
        ex = jnp.exp(logits - jnp.max(logits, axis=-1, keepdims=True))
        probs = ex / jnp.sum(ex, axis=-1, keepdims=True)
        sel = probs + rb_ref[...]
        grp = lane // 4

        def top2(vals):
            m1 = jnp.max(vals, axis=-1, keepdims=True)
            i1 = jnp.min(jnp.where(vals == m1, lanef, 999.0), axis=-1, keepdims=True)
            rest = jnp.where(lanef == i1, neg, vals)
            m2 = jnp.max(rest, axis=-1, keepdims=True)
            i2 = jnp.min(jnp.where(rest == m2, lanef, 999.0), axis=-1, keepdims=True)
            return m1, i1, m2, i2

        best = None
        for gidx in range(4):
            m1, _, m2, _ = top2(jnp.where(grp == gidx, sel, neg))
            score = m1 + m2
            if best is None:
                best, gsel = score, jnp.zeros_like(score, dtype=jnp.int32)
            else:
                better = score > best
                gsel = jnp.where(better, gidx, gsel)
                best = jnp.where(better, score, best)
        _, i1, _, i2 = top2(jnp.where(grp == gsel, sel, neg))
        w1 = jnp.sum(jnp.where(lanef == i1, probs, 0.0), axis=-1, keepdims=True)
        w2 = jnp.sum(jnp.where(lanef == i2, probs, 0.0), axis=-1, keepdims=True)
        cmb_ref[0, rs, :] = ((jnp.where(lanef == i1, w1, 0.0) + jnp.where(lanef == i2, w2, 0.0)) / (w1 + w2)
                             + jnp.where(lane == N_EXPERTS + gsel, 1.0, 0.0))
        yield

    _alternate(rows_of(0), rows_of(1))


def _merge_call(x, h, branches, mod, wg, bg, wbr, wo, l1g, l1b, rw, rb, tm=512):
    bsz, t_len, _ = x.shape
    tok = lambda w: pl.BlockSpec((1, tm, w), lambda b, t: (b, t, 0))
    const = lambda shape: pl.BlockSpec(shape, lambda b, t: (0,) * len(shape), pipeline_mode=pl.Buffered(1))
    return pl.pallas_call(
        _merge_kernel,
        grid=(bsz, t_len // tm),
        in_specs=[tok(D_MODEL), tok(D_MODEL)] + [tok(BRANCH_W)] * 4
        + [pl.BlockSpec((1, N_ADA, D_MODEL), lambda b, t: (b, 0, 0)),
           const((4, D_MODEL, D_MODEL)), const((4, D_MODEL)), const((4, BRANCH_W, D_MODEL)), const((D_MODEL, D_MODEL)),
           const((1, D_MODEL)), const((1, D_MODEL)), const((D_MODEL, LANES)), const((1, LANES))],
        out_specs=[tok(D_MODEL), tok(D_MODEL), tok(LANES)],
        out_shape=[jax.ShapeDtypeStruct((bsz, t_len, D_MODEL), F32),
                   jax.ShapeDtypeStruct((bsz, t_len, D_MODEL), BF16),
                   jax.ShapeDtypeStruct((bsz, t_len, LANES), F32)],
        compiler_params=_cparams(("parallel", "parallel")),
        name="merge",
    )(x, h, *branches, mod, wg, bg, wbr, wo, l1g, l1b, rw, rb)


def _moe_kernel(h_ref, cmb_ref, x_ref, mod_ref, wg_ref, wu_ref, wd_ref, l2g_ref, l2b_ref, o_ref,
                xs_s, cs_s, ys_s, pos_s, seg_s, *, tm):
    e = pl.program_id(2)
    nrow = tm + N_GROUPS * MOE_SEG
    lane = lax.broadcasted_iota(jnp.int32, (1, LANES), 1)

    @pl.when(e == 0)
    def _():
        cmb = cmb_ref[0]
        gh = jnp.where((lane >= N_EXPERTS) & (lane < N_EXPERTS + N_GROUPS), cmb, 0.0)
        tr = lax.broadcasted_iota(jnp.int32, (tm, tm), 0)
        tc = lax.broadcasted_iota(jnp.int32, (tm, tm), 1)
        rank = _dot(jnp.where(tc < tr, 1.0, 0.0).astype(BF16), gh.astype(BF16))
        counts = jnp.sum(gh, axis=0, keepdims=True)
        padded = jnp.floor((counts + (MOE_SEG - 1)) * (1.0 / MOE_SEG)) * MOE_SEG
        lr = lax.broadcasted_iota(jnp.int32, (LANES, LANES), 0)
        lc = lax.broadcasted_iota(jnp.int32, (LANES, LANES), 1)
        before = jnp.where(lr < lc, 1.0, 0.0).astype(BF16)
        start = _dot_split_lhs(jnp.broadcast_to(padded, (8, LANES)), before, 3)[0:1, :]
        pos = jnp.sum(gh * (start + rank), axis=-1, keepdims=True)
        posb = jnp.broadcast_to(pos, (tm, LANES))
        pos_s[...] = posb
        pos_row = posb.T[0:1, :]
        perm = jnp.where(lax.broadcasted_iota(jnp.int32, (nrow, tm), 0).astype(F32) == pos_row, 1.0, 0.0)
        perm = perm.astype(BF16)
        xs_s[...] = _dot(perm, h_ref[0]).astype(BF16)
        cs_s[...] = _dot_split_rhs(perm, cmb, 3)
        ys_s[...] = jnp.zeros_like(ys_s)
        for g in range(N_GROUPS):
            pick = lane == N_EXPERTS + g
            seg_s[g] = jnp.sum(jnp.where(pick, start, 0.0)).astype(jnp.int32)
            seg_s[N_GROUPS + g] = jnp.sum(jnp.where(pick, padded, 0.0)).astype(jnp.int32) // MOE_SEG

    grp = e // (N_EXPERTS // N_GROUPS)
    seg_start = seg_s[grp]

    nseg = seg_s[N_GROUPS + grp]

    def block(first_seg, nrows):
        rows = pl.ds(pl.multiple_of(seg_start + first_seg * MOE_SEG, MOE_SEG), nrows)
        xb = xs_s[rows, :]
        a = _dot(xb, wg_ref[0])
        u = _dot(xb, wu_ref[0])
        he = (a * _sigmoid(a) * u).astype(BF16)
        ce = jnp.sum(jnp.where(lane == e, cs_s[rows, :], 0.0), axis=-1, keepdims=True)
        ys_s[rows, :] += ce * _dot(he, wd_ref[0])

    def double_block(i, carry):
        block(2 * i, 2 * MOE_SEG)
        return carry

    lax.fori_loop(0, nseg // 2, double_block, 0)

    @pl.when(nseg % 2 == 1)
    def _():
        block(nseg - 1, MOE_SEG)

    @pl.when(e == N_EXPERTS - 1)
    def _():
        back = jnp.where(lax.broadcasted_iota(jnp.int32, (tm, nrow), 1).astype(F32) == pos_s[:, 0:1], 1.0, 0.0)
        y_hi, y_lo = _split_bf16(ys_s[...], 2)
        back = back.astype(BF16)
        y = _dot(back, y_hi) + _dot(back, y_lo)
        gt2 = mod_ref[0, 5:6, :]
        o_ref[0] = _ln(DN_ALPHA * x_ref[0] + gt2 * y) * l2g_ref[...] + l2b_ref[...]


def _moe_call(h2, cmb, x1, mod, wg, wu, wd, l2g, l2b, tm=1024):
    bsz, t_len, _ = x1.shape
    tok = lambda w: pl.BlockSpec((1, tm, w), lambda b, t, e: (b, t, 0))
    nrow = tm + N_GROUPS * MOE_SEG
    return pl.pallas_call(
        functools.partial(_moe_kernel, tm=tm),
        grid=(bsz, t_len // tm, N_EXPERTS),
        in_specs=[tok(D_MODEL), tok(LANES), tok(D_MODEL),
                  pl.BlockSpec((1, N_ADA, D_MODEL), lambda b, t, e: (b, 0, 0)),
                  pl.BlockSpec((1, D_MODEL, D_FF_EXPERT), lambda b, t, e: (e, 0, 0)),
                  pl.BlockSpec((1, D_MODEL, D_FF_EXPERT), lambda b, t, e: (e, 0, 0)),
                  pl.BlockSpec((1, D_FF_EXPERT, D_MODEL), lambda b, t, e: (e, 0, 0)),
                  pl.BlockSpec((1, D_MODEL), lambda b, t, e: (0, 0)),
                  pl.BlockSpec((1, D_MODEL), lambda b, t, e: (0, 0))],
        out_specs=tok(D_MODEL),
        out_shape=jax.ShapeDtypeStruct((bsz, t_len, D_MODEL), F32),
        scratch_shapes=[pltpu.VMEM((nrow, D_MODEL), BF16), pltpu.VMEM((nrow, LANES), F32),
                        pltpu.VMEM((nrow, D_MODEL), F32), pltpu.VMEM((tm, LANES), F32),
                        pltpu.SMEM((2 * N_GROUPS,), jnp.int32)],
        compiler_params=_cparams(("parallel", "parallel", "arbitrary")),
        name="moe",
    )(h2, cmb, x1, mod, wg, wu, wd, l2g, l2b)


def _pad_cols(w, n):
    return jnp.pad(w, ((0, 0), (0, n - w.shape[1])))


def kernel(x, c, ada_w, ada_b, w_in, gla_alpha_up, gla_alpha_b, gla_norm_g, rwkv_mu, rwkv_w0, rwkv_w2, rwkv_a0, rwkv_a2, rwkv_g2, rwkv_k_k, rwkv_k_a, rwkv_r_k, rwkv_ln_g, rwkv_ln_b, fox_f_bias, hgrn_lb_logits, hgrn_norm_g, w_br, w_gate, b_gate, w_o, ln1_g, ln1_b, router_w, router_b, exp_w_gate, exp_w_up, exp_w_down, ln2_g, ln2_b):
    mod_all = _ada_call(c, ada_w, ada_b)
    rw = _pad_cols(router_w, LANES)
    rb = _pad_cols(router_b.reshape(1, N_EXPERTS), LANES)
    o1, o2, o3 = GLA_COLS, GLA_COLS + RWKV_COLS, GLA_COLS + RWKV_COLS + FOX_COLS
    for i in range(DEPTH):
        mod = mod_all[i]
        wi = w_in[i]
        wa = _pad_cols(wi[:, :o1], PAD_COLS).astype(BF16)
        wb = wi[:, o1:o2].astype(BF16)
        wc = _pad_cols(wi[:, o2:o3], PAD_COLS).astype(BF16)
        wd = wi[:, o3:].astype(BF16)
        h, pa, pb, pcq, pckv, pcf, pd = _inproj_call(x, mod, wa, wb, wc, wd)
        br_a = _gla_call(pa, gla_alpha_up[i], gla_alpha_b[i], gla_norm_g[i])
        br_b = _rwkv_call(pb, rwkv_mu[i], rwkv_w0[i], rwkv_w2[i], rwkv_a0[i], rwkv_a2[i], rwkv_g2[i],
                          rwkv_k_k[i], rwkv_k_a[i], rwkv_r_k[i], rwkv_ln_g[i], rwkv_ln_b[i])
        br_c = _fox_call(pcq, pckv, _foxf_call(pcf, fox_f_bias[i]))
        br_d = _hgrn_call(pd, hgrn_lb_logits, hgrn_norm_g[i], i)
        x1, h2, cmb = _merge_call(
            x, h, (br_a, br_b, br_c, br_d), mod, w_gate[i].astype(BF16), b_gate[i], w_br[i].astype(BF16),
            w_o[i].astype(BF16), ln1_g[i].reshape(1, -1), ln1_b[i].reshape(1, -1), rw, rb)
        x = _moe_call(h2, cmb, x1, mod, exp_w_gate[i].astype(BF16), exp_w_up[i].astype(BF16),
                      exp_w_down[i].astype(BF16), ln2_g[i].reshape(1, -1), ln2_b[i].reshape(1, -1))
    return x
```

```python
import functools

import jax
import jax.numpy as jnp
from jax import lax
from jax.experimental import pallas as pl
from jax.experimental.pallas import tpu as pltpu

F32 = jnp.float32
BF16 = jnp.bfloat16
HI = lax.Precision.HIGHEST

D_MODEL = 1024
DEPTH = 2
N_ADA = 6
BRANCH_W = 512
GLA_HEADS = 4
GLA_DK = 64
GLA_KW = 256
GLA_GATE_RANK = 16
GLA_LOGIT_NORM = 16.0
RWKV_HEAD = 64
RWKV_COLS = 1792
RWKV_GN_EPS = 64e-5
FOX_HEADS = 8
FOX_DH = 64
HGRN_KW = 512
N_EXPERTS = 16
N_GROUPS = 4
D_FF_EXPERT = 512
MOE_SEG = 128
MOE_EPS = 2
DN_ALPHA = (2.0 * DEPTH) ** 0.25
GLA_COLS = 1552
FOX_COLS = 1544
HGRN_COLS = 2048
PAD_COLS = 1664

LANES = 128
CHUNK = 64
SUB = 16
FACTORED_MAX_DECAY = 80.0
FACTORED_MAX_ABS = 1e3
VMEM_LIMIT = 56 * 1024 * 1024


def _cparams(sem):
    return pltpu.CompilerParams(dimension_semantics=sem, vmem_limit_bytes=VMEM_LIMIT)


def _sigmoid(x):
    return 1.0 / (1.0 + jnp.exp(-x))


def _log_sigmoid(x):
    return jnp.minimum(x, 0.0) - jnp.log(1.0 + jnp.exp(-jnp.abs(x)))


def _ln(x, eps=1e-5):
    mu = jnp.mean(x, axis=-1, keepdims=True)
    xc = x - mu
    var = jnp.mean(xc * xc, axis=-1, keepdims=True)
    return xc * lax.rsqrt(var + eps)


def _tri(n, strict=False):
    r = lax.broadcasted_iota(jnp.int32, (n, n), 0)
    c = lax.broadcasted_iota(jnp.int32, (n, n), 1)
    return jnp.where((c < r) if strict else (c <= r), 1.0, 0.0).astype(F32)


def _dot(a, b, prec=None):
    return jnp.dot(a, b, preferred_element_type=F32, precision=prec)


def _dot_nt(a, b, prec=None):
    return lax.dot_general(a, b, (((1,), (1,)), ((), ())), preferred_element_type=F32, precision=prec)


def _alternate(*gens):
    live = list(gens)
    while live:
        for gen in list(live):
            if next(gen, "done") == "done":
                live.remove(gen)


def _dot_tn(a, b, prec=None):
    return lax.dot_general(a, b, (((0,), (0,)), ((), ())), preferred_element_type=F32, precision=prec)


def _ada_kernel(c_ref, w_ref, b_ref, o_ref):
    c = c_ref[...]
    cond = c * _sigmoid(c)
    o_ref[0] = _dot(cond, w_ref[0], HI) + b_ref[0]


def _ada_call(c, ada_w, ada_b):
    bsz = c.shape[0]
    n = N_ADA * D_MODEL
    bn = 1536
    out = pl.pallas_call(
        _ada_kernel,
        grid=(DEPTH, n // bn),
        in_specs=[
            pl.BlockSpec((bsz, D_MODEL), lambda i, j: (0, 0)),
            pl.BlockSpec((1, D_MODEL, bn), lambda i, j: (i, 0, j)),
            pl.BlockSpec((1, 1, bn), lambda i, j: (i, 0, j)),
        ],
        out_specs=pl.BlockSpec((1, bsz, bn), lambda i, j: (i, 0, j)),
        out_shape=jax.ShapeDtypeStruct((DEPTH, bsz, n), F32),
        compiler_params=_cparams(("parallel", "parallel")),
        name="ada_mod",
    )(c, ada_w, ada_b.reshape(DEPTH, 1, n))
    return out.reshape(DEPTH, bsz, N_ADA, D_MODEL)


def _inproj_kernel(x_ref, mod_ref, wa_ref, wb_ref, wc_ref, wd_ref,
                   h_ref, oa_ref, ob_ref, ocq_ref, ockv_ref, ocf_ref, od_ref):
    x = x_ref[0]
    sh = mod_ref[0, 0:1, :]
    sc = mod_ref[0, 1:2, :]
    hb = (_ln(x) * (1.0 + sc) + sh).astype(BF16)
    h_ref[0] = hb
    oa_ref[0] = _dot(hb, wa_ref[...])
    ob_ref[0] = _dot(hb, wb_ref[...])
    oc = _dot(hb, wc_ref[...])
    w = BRANCH_W
    ocq_ref[0] = oc[:, 0:w]
    ockv_ref[0] = oc[:, w:3 * w].astype(BF16)
    ocf_ref[0] = oc[:, 3 * w:3 * w + LANES]
    od_ref[0] = _dot(hb, wd_ref[...])


def _inproj_call(x, mod, wa, wb, wc, wd, tm=256):
    bsz, t_len, _ = x.shape
    widths = (wa.shape[1], wb.shape[1], wc.shape[1], wd.shape[1])
    outs = [(D_MODEL, BF16), (widths[0], F32), (widths[1], F32), (BRANCH_W, F32), (2 * BRANCH_W, BF16),
            (LANES, F32), (widths[3], F32)]

    def wspec(w):
        return pl.BlockSpec((D_MODEL, w), lambda b, t: (0, 0), pipeline_mode=pl.Buffered(1))

    return pl.pallas_call(
        _inproj_kernel,
        grid=(bsz, t_len // tm),
        in_specs=[pl.BlockSpec((1, tm, D_MODEL), lambda b, t: (b, t, 0)),
                  pl.BlockSpec((1, N_ADA, D_MODEL), lambda b, t: (b, 0, 0))] + [wspec(w) for w in widths],
        out_specs=[pl.BlockSpec((1, tm, w), lambda b, t: (b, t, 0)) for w, _ in outs],
        out_shape=[jax.ShapeDtypeStruct((bsz, t_len, w), dt) for w, dt in outs],
        compiler_params=_cparams(("parallel", "parallel")),
        name="inproj",
    )(x, mod, wa, wb, wc, wd)


def _gla_kernel(*refs, mode, layer, hp, qscale, tm, ngrp):
    if mode == "gla":
        q_ref, k_ref, v_ref, gate_ref, al_ref, aup_ref, ab_ref, ng_ref, sel_ref, o_ref = refs[:10]
    else:
        q_ref, k_ref, v_ref, gate_ref, lbl_ref, ng_ref, sel_ref, o_ref = refs[:8]
    st_ref, q_s, k_s, b_s, qe_s, kd_s, pw_s = refs[-7:]
    c = CHUNK
    nb = c // SUB
    dk = LANES // hp

    @pl.when(pl.program_id(1) == 0)
    def _():
        st_ref[...] = jnp.zeros_like(st_ref)

    q = q_ref[0]
    if qscale != 1.0:
        q = q * qscale
    kin = k_ref[0]
    if mode == "gla":
        g = _log_sigmoid(_dot(al_ref[0], aup_ref[...], HI) + ab_ref[...]) / GLA_LOGIT_NORM
        k = kin
    else:
        lg = lbl_ref[...]
        mx = jnp.max(lg, axis=0, keepdims=True)
        ex = jnp.exp(lg - mx)
        p = ex / jnp.sum(ex, axis=0, keepdims=True)
        cum = p[0:1, :]
        for j in range(1, layer + 1):
            cum = cum + p[j:j + 1, :]
        lb = cum - p[0:1, :]
        g = jnp.log(lb + (1.0 - lb) * _sigmoid(kin))
        k = (1.0 - lb) * _sigmoid(-kin)
    tr = lax.broadcasted_iota(jnp.int32, (tm, tm), 0)
    tc = lax.broadcasted_iota(jnp.int32, (tm, tm), 1)
    same = (tr // c) == (tc // c)
    b = _dot_split_rhs(jnp.where((tc <= tr) & same, 1.0, 0.0).astype(BF16), g, 3)
    nch = tm // c
    b_end = jnp.concatenate([jnp.broadcast_to(b[(ci + 1) * c - 1:(ci + 1) * c, :], (c, b.shape[1]))
                             for ci in range(nch)], axis=0)
    qe = (q * jnp.exp(b)).astype(BF16)
    kd = (k * jnp.exp(b_end - b)).astype(BF16)

    lane = lax.broadcasted_iota(jnp.int32, (1, LANES), 1)
    hmask = [((lane >= h * dk) & (lane < (h + 1) * dk)) for h in range(hp)]
    row = lax.broadcasted_iota(jnp.int32, (c, c), 0)
    col = lax.broadcasted_iota(jnp.int32, (c, c), 1)
    off_mask = (col // SUB) < (row // SUB)
    diag_mask = ((col // SUB) == (row // SUB)) & (col <= row)
    ng = ng_ref[...]

    def head_lanes(x, h):
        return x if hp == 1 else jnp.where(hmask[h], x, jnp.zeros_like(x))

    def finish(rows, hv, o):
        o = o * lax.rsqrt(jnp.mean(o * o, axis=-1, keepdims=True) + 1e-6) * ng
        gt = gate_ref[0, rows, hv * LANES:(hv + 1) * LANES]
        o_ref[0, rows, hv * LANES:(hv + 1) * LANES] = o * (gt * _sigmoid(gt))

    def intra_exact(rows):
        qhs, khats = {}, []
        for gi in range(ngrp):
            gl = slice(gi * LANES, (gi + 1) * LANES)
            qc = q_s[rows, gl]
            kc = k_s[rows, gl]
            bc = b_s[rows, gl]
            rblk = jnp.concatenate([jnp.broadcast_to(bc[m * SUB:m * SUB + 1, :], (SUB, LANES)) for m in range(nb)],
                                   axis=0)
            qh = (qc * jnp.exp(bc - rblk)).astype(BF16)
            for h in range(hp):
                qhs[gi * hp + h] = head_lanes(qh, h)
            khats.append([None] + [(kc * jnp.exp(jnp.minimum(bc[m * SUB:m * SUB + 1, :] - bc, 0.0))).astype(BF16)
                                   for m in range(1, nb)])
            for jj in range(SUB):
                kb = jnp.concatenate([jnp.broadcast_to(kc[m * SUB + jj:m * SUB + jj + 1, :], (SUB, LANES))
                                      for m in range(nb)], axis=0)
                bb = jnp.concatenate([jnp.broadcast_to(bc[m * SUB + jj:m * SUB + jj + 1, :], (SUB, LANES))
                                      for m in range(nb)], axis=0)
                pw_s[gi * c:(gi + 1) * c, jj * LANES:(jj + 1) * LANES] = (
                    qc * kb * jnp.exp(jnp.minimum(bc - bb, 0.0))).astype(BF16)
        a_offs = {}
        for hv in range(ngrp * hp):
            a_offs[hv] = jnp.concatenate(
                [jnp.zeros((SUB, c), F32)]
                + [_dot_nt(qhs[hv][m * SUB:(m + 1) * SUB, :], khats[hv // hp][m]) for m in range(1, nb)], axis=0)
        a_diags = [_dot(pw_s[...], sel_ref[h]) for h in range(hp)]
        out = {}
        for hv in range(ngrp * hp):
            gi, h = hv // hp, hv % hp
            out[hv] = jnp.where(off_mask, a_offs[hv], jnp.where(diag_mask, a_diags[h][gi * c:(gi + 1) * c], 0.0))
        return out

    def chunk(ci, carry):
        r0 = pl.multiple_of(ci * c, c)
        rows = pl.ds(r0, c)
        sts, o_int, upds, vbs = [], {}, {}, {}
        for gi in range(ngrp):
            gl = slice(gi * LANES, (gi + 1) * LANES)
            sts.append(st_ref[gi])
            stb = sts[gi].astype(BF16)
            for h in range(hp):
                hv = gi * hp + h
                vbs[hv] = v_ref[0, rows, hv * LANES:(hv + 1) * LANES].astype(BF16)
                o_int[hv] = _dot_nt(head_lanes(qe_s[rows, gl], h), stb)
                upds[hv] = _dot_tn(vbs[hv], kd_s[rows, gl])
        attn = intra_exact(rows)
        for hv in range(ngrp * hp):
            finish(rows, hv, o_int[hv] + _dot(attn[hv].astype(BF16), vbs[hv]))
        for gi in range(ngrp):
            gl = slice(gi * LANES, (gi + 1) * LANES)
            b_last = b_s[pl.ds(pl.multiple_of(r0 + c - 8, 8), 8), gl][7:8, :]
            new_st = sts[gi] * jnp.exp(b_last)
            for h in range(hp):
                new_st = new_st + head_lanes(upds[gi * hp + h], h)
            st_ref[gi] = new_st
        return carry

    small_decay = jnp.max(jnp.maximum(-b * (1.0 / FACTORED_MAX_DECAY),
                                      jnp.maximum(jnp.abs(q), jnp.abs(k)) * (1.0 / FACTORED_MAX_ABS))) < 1.0

    @pl.when(small_decay)
    def _():
        heads = [(ci, hv) for ci in range(nch) for hv in range(ngrp * hp)]
        b0 = jnp.concatenate([jnp.broadcast_to(b[ci * c:ci * c + 1, :], (c, b.shape[1])) for ci in range(nch)], axis=0)
        bz = b - b0
        qz = (q * jnp.exp(bz)).astype(BF16)
        kz = (k * jnp.exp(-bz)).astype(BF16)
        sl = lambda ci, gi: (slice(ci * c, (ci + 1) * c), slice(gi * LANES, (gi + 1) * LANES))
        vbs, upds, attn, o_intra, o_int = {}, {}, {}, {}, {}
        for ci, hv in heads:
            vbs[ci, hv] = v_ref[0, ci * c:(ci + 1) * c, hv * LANES:(hv + 1) * LANES].astype(BF16)
            upds[ci, hv] = _dot_tn(vbs[ci, hv], kd[sl(ci, hv // hp)])
        for ci, hv in heads:
            attn[ci, hv] = jnp.where(col <= row, _dot_nt(head_lanes(qz[sl(ci, hv // hp)], hv % hp),
                                                           kz[sl(ci, hv // hp)]), 0.0).astype(BF16)
        for ci, hv in heads:
            o_intra[ci, hv] = _dot(attn[ci, hv], vbs[ci, hv])
        sts = [[st_ref[gi]] for gi in range(ngrp)]
        for ci in range(nch):
            for gi in range(ngrp):
                new_st = sts[gi][ci] * jnp.exp(b[(ci + 1) * c - 1:(ci + 1) * c, gi * LANES:(gi + 1) * LANES])
                for h in range(hp):
                    new_st = new_st + head_lanes(upds[ci, gi * hp + h], h)
                sts[gi].append(new_st)
        for gi in range(ngrp):
            st_ref[gi] = sts[gi][nch]
        for ci, hv in heads:
            o_int[ci, hv] = _dot_nt(head_lanes(qe[sl(ci, hv // hp)], hv % hp), sts[hv // hp][ci].astype(BF16))
        for ci, hv in heads:
            finish(slice(ci * c, (ci + 1) * c), hv, o_int[ci, hv] + o_intra[ci, hv])

    @pl.when(jnp.logical_not(small_decay))
    def _():
        q_s[...] = q
        k_s[...] = k
        b_s[...] = b
        qe_s[...] = qe
        kd_s[...] = kd
        lax.fori_loop(0, nch, chunk, 0)


def _pair_select(hp):
    dk = LANES // hp
    r = jnp.arange(SUB * LANES)
    jj = r // LANES
    head = (r % LANES) // dk
    cj = jnp.arange(CHUNK) % SUB
    sel = (jj[None, :, None] == cj[None, None, :]) & (head[None, :, None] == jnp.arange(hp)[:, None, None])
    return sel.astype(BF16)


def _gla_scratch(tm, ngrp):
    w = ngrp * LANES
    return ([pltpu.VMEM((ngrp, LANES, LANES), F32)] + [pltpu.VMEM((tm, w), F32) for _ in range(3)]
            + [pltpu.VMEM((tm, w), BF16) for _ in range(2)] + [pltpu.VMEM((ngrp * CHUNK, SUB * LANES), BF16)])


def _gla_call(proj, alpha_up, alpha_b, norm_g, tm=256):
    bsz, t_len, _ = proj.shape
    hp = 2
    ngrp = GLA_HEADS // hp
    aup = jnp.zeros((LANES, GLA_KW), F32).at[:GLA_GATE_RANK].set(alpha_up)
    blk = lambda w, j: pl.BlockSpec((1, tm, w), lambda b, t: (b, t, j))
    const = lambda shape: pl.BlockSpec(shape, lambda b, t: (0,) * len(shape))
    kern = functools.partial(_gla_kernel, mode="gla", layer=0, hp=hp, qscale=GLA_DK ** -0.5, tm=tm, ngrp=ngrp)
    return pl.pallas_call(
        kern,
        grid=(bsz, t_len // tm),
        in_specs=[blk(GLA_KW, 0), blk(GLA_KW, 1), blk(BRANCH_W, 1), blk(BRANCH_W, 2), blk(LANES, 12),
                  const((LANES, GLA_KW)), const((1, GLA_KW)), const((1, LANES)), const((hp, SUB * LANES, CHUNK))],
        out_specs=blk(BRANCH_W, 0),
        out_shape=jax.ShapeDtypeStruct((bsz, t_len, BRANCH_W), F32),
        scratch_shapes=_gla_scratch(tm, ngrp),
        compiler_params=_cparams(("parallel", "arbitrary")),
        name="gla",
    )(proj, proj, proj, proj, proj, aup, alpha_b.reshape(1, GLA_KW), norm_g.reshape(1, LANES), _pair_select(hp))


def _hgrn_call(proj, lb_logits, norm_g, layer, tm=256):
    bsz, t_len, _ = proj.shape
    ngrp = HGRN_KW // LANES
    blk = lambda j: pl.BlockSpec((1, tm, HGRN_KW), lambda b, t: (b, t, j))
    const = lambda shape: pl.BlockSpec(shape, lambda b, t: (0,) * len(shape))
    kern = functools.partial(_gla_kernel, mode="hgrn", layer=layer, hp=1, qscale=1.0, tm=tm, ngrp=ngrp)
    return pl.pallas_call(
        kern,
        grid=(bsz, t_len // tm),
        in_specs=[blk(0), blk(1), blk(2), blk(3), const((DEPTH, HGRN_KW)), const((1, LANES)),
                  const((1, SUB * LANES, CHUNK))],
        out_specs=blk(0),
        out_shape=jax.ShapeDtypeStruct((bsz, t_len, BRANCH_W), F32),
        scratch_shapes=_gla_scratch(tm, ngrp),
        compiler_params=_cparams(("parallel", "arbitrary")),
        name="hgrn",
    )(proj, proj, proj, proj, lb_logits, norm_g.reshape(1, LANES), _pair_select(1))


def _split_bf16(x, n):
    parts, rest = [], x
    for i in range(n):
        part = rest.astype(BF16)
        parts.append(part)
        if i + 1 < n:
            rest = rest - part.astype(F32)
    return parts


def _dot_split_rhs(a01, x, n):
    out = None
    for part in _split_bf16(x, n):
        term = _dot(a01, part)
        out = term if out is None else out + term
    return out


def _dot_split_lhs(x, b01, n):
    out = None
    for part in _split_bf16(x, n):
        term = _dot(part, b01)
        out = term if out is None else out + term
    return out


def _rwkv_kernel(cols_ref, mu_ref, w0_ref, w2_ref, a0_ref, a2_ref, g2_ref, kk_ref, ka_ref, rk_ref, lng_ref, lnb_ref,
                 o_ref, carry_ref, st_ref, rt_s, at_s, bt_s, kt_s, vb_s, w2_s, eg_s, y_s, w1_s, tops_s, *, tm):
    w = BRANCH_W
    c = CHUNK
    npair = w // LANES

    @pl.when(pl.program_id(1) == 0)
    def _():
        carry_ref[...] = jnp.zeros_like(carry_ref)
        st_ref[...] = jnp.zeros_like(st_ref)

    cols = cols_ref[0]
    rowi = lax.broadcasted_iota(jnp.int32, cols.shape, 0)
    prev = jnp.where(rowi == 0, carry_ref[0:1, :], pltpu.roll(cols, 1, 0))
    carry_ref[0:1, :] = cols[tm - 1:tm, :]
    xs = cols + (prev - cols) * mu_ref[...]
    r = xs[:, 0:w]
    k = xs[:, w:2 * w]
    v = xs[:, 2 * w:3 * w]
    wl = xs[:, 3 * w:3 * w + LANES]
    gl = xs[:, 3 * w + LANES:3 * w + 2 * LANES]
    w_raw = _log_sigmoid(w0_ref[...] + _dot(jnp.tanh(wl).astype(BF16), w2_ref[...])) - 0.5
    lw = -jnp.exp(w_raw)
    a = _sigmoid(a0_ref[...] + _dot(wl.astype(BF16), a2_ref[...]))
    g = _dot(_sigmoid(gl).astype(BF16), g2_ref[...])

    hr = lax.broadcasted_iota(jnp.int32, (LANES, LANES), 0) // RWKV_HEAD
    hc = lax.broadcasted_iota(jnp.int32, (LANES, LANES), 1) // RWKV_HEAD
    bdmask = hr == hc
    bd = jnp.where(bdmask, 1.0, 0.0).astype(BF16)

    def head_sum(t, n):
        return jnp.concatenate([_dot_split_lhs(t[:, p * LANES:(p + 1) * LANES], bd, n) for p in range(npair)], axis=1)

    kk = k * kk_ref[...]
    kk = kk / jnp.maximum(jnp.sqrt(head_sum(kk * kk, 3)), 1e-12)
    k = k * (1.0 + (a - 1.0) * ka_ref[...])

    tr = lax.broadcasted_iota(jnp.int32, (tm, tm), 0)
    tc = lax.broadcasted_iota(jnp.int32, (tm, tm), 1)
    tri = jnp.where((tc <= tr) & ((tr // c) == (tc // c)), 1.0, 0.0).astype(BF16)
    gam = _dot_split_rhs(tri, lw, 3)
    eg = jnp.exp(gam)
    ieg = jnp.exp(-gam)
    rt_s[...] = (r * eg).astype(BF16)
    at_s[...] = (-kk * jnp.exp(gam - lw)).astype(BF16)
    bt_s[...] = (kk * a * ieg).astype(BF16)
    kt_s[...] = (k * ieg).astype(BF16)
    vb_s[...] = v.astype(BF16)
    eg_s[...] = eg

    lane = lax.broadcasted_iota(jnp.int32, (1, LANES), 1)
    lo = lane < RWKV_HEAD
    hmask = [lo, jnp.logical_not(lo)]
    row = lax.broadcasted_iota(jnp.int32, (c, LANES), 0)
    col = lax.broadcasted_iota(jnp.int32, (c, LANES), 1)
    top_mask = (col % c) <= row
    ak_mask = (col >= c) & ((col - c) < row)
    r64 = lax.broadcasted_iota(jnp.int32, (c, c), 0)
    c64 = lax.broadcasted_iota(jnp.int32, (c, c), 1)
    strict = c64 < r64
    eye = jnp.where(c64 == r64, 1.0, 0.0).astype(F32)

    def solve(j):
        chains = []
        for cc in range(2):
            r0 = (2 * j + cc) * c
            for p in range(npair):
                ls = slice(p * LANES, (p + 1) * LANES)
                at = at_s[pl.ds(r0, c), ls]
                vb = vb_s[pl.ds(r0, c), ls]
                ra = jnp.concatenate([rt_s[pl.ds(r0, c), ls], at], axis=0)
                bk = jnp.concatenate([bt_s[pl.ds(r0, c), ls], kt_s[pl.ds(r0, c), ls]], axis=0)
                vv = jnp.concatenate([vb, vb], axis=0)
                for h in range(2):
                    x = _dot_nt(jnp.where(hmask[h], ra, jnp.zeros_like(ra)), bk)
                    tops_s[pl.ds(r0, c), (2 * p + h) * LANES:(2 * p + h + 1) * LANES] = (
                        jnp.where(top_mask, x[0:c], 0.0).astype(BF16))
                    chains.append(dict(r0=r0, p=p, h=h, vv=vv, atm=jnp.where(hmask[h], at, jnp.zeros_like(at)),
                                       aab=jnp.where(strict, x[c:2 * c, 0:c], 0.0),
                                       akm=jnp.where(ak_mask, x[c:2 * c], 0.0).astype(BF16)))
            yield
        for ch in chains:
            ab = ch["aab"].astype(BF16)
            ch["xp"] = _dot(ab, ab)
            ch["t"] = eye + ch["aab"]
        yield
        for ch in chains:
            ch["akv"] = _dot(ch["akm"], ch["vv"])
        yield
        for stage in range(5):
            for ch in chains:
                xb = ch["xp"].astype(BF16)
                if stage < 4:
                    res = _dot(jnp.concatenate([ch["t"].astype(BF16), xb], axis=0), xb)
                    ch["t"] = ch["t"] + res[0:c]
                    ch["xp"] = res[c:2 * c]
                else:
                    ch["t"] = ch["t"] + _dot(ch["t"].astype(BF16), xb)
            yield
        for ch in chains:
            z = jnp.concatenate([ch["akv"].astype(BF16), ch["atm"]], axis=1)
            ch["tz"] = _dot(ch["t"].astype(BF16), z)
        for i in range(0, len(chains), 2):
            c0, c1 = chains[i], chains[i + 1]
            ls = slice(c0["p"] * LANES, (c0["p"] + 1) * LANES)
            w1_s[pl.ds(c0["r0"], c), ls] = jnp.where(lo, c0["tz"][:, 0:LANES], c1["tz"][:, 0:LANES])
            w2_s[pl.ds(c0["r0"], c), ls] = (c0["tz"][:, LANES:2 * LANES] + c1["tz"][:, LANES:2 * LANES]).astype(BF16)
        yield

    def scan(ci_):
        r0 = ci_ * c
        pts, rps, bks, vbs = [], [], [], []
        for p in range(npair):
            ls = slice(p * LANES, (p + 1) * LANES)
            pt = st_ref[p]
            rw = jnp.concatenate([rt_s[pl.ds(r0, c), ls], w2_s[pl.ds(r0, c), ls]], axis=0)
            pts.append(pt)
            rps.append(_dot_nt(rw, pt.astype(BF16)))
        yield
        uvs = []
        for p in range(npair):
            ls = slice(p * LANES, (p + 1) * LANES)
            u = w1_s[pl.ds(r0, c), ls] + rps[p][c:2 * c]
            vb = vb_s[pl.ds(r0, c), ls]
            uvs.append(jnp.concatenate([u.astype(BF16), vb], axis=0))
            bks.append(jnp.concatenate([bt_s[pl.ds(r0, c), ls], kt_s[pl.ds(r0, c), ls]], axis=0))
        for p in range(npair):
            ls = slice(p * LANES, (p + 1) * LANES)
            upd = _dot_tn(uvs[p], bks[p])
            eg_last = eg_s[pl.ds(r0 + c - 1, 1), ls]
            st_ref[p] = jnp.where(bdmask, (pts[p] + upd) * eg_last, 0.0)
        yield
        for p in range(npair):
            ls = slice(p * LANES, (p + 1) * LANES)
            y0 = _dot(tops_s[pl.ds(r0, c), (2 * p) * LANES:(2 * p + 1) * LANES], uvs[p])
            y1 = _dot(tops_s[pl.ds(r0, c), (2 * p + 1) * LANES:(2 * p + 2) * LANES], uvs[p])
            y_s[pl.ds(r0, c), ls] = rps[p][0:c] + jnp.where(lo, y0, y1)
        yield

    def scan_pair(j):
        yield from scan(2 * j)
        yield from scan(2 * j + 1)

    npairs_t = tm // (2 * c)
    _alternate(solve(0))
    for j in range(1, npairs_t):
        _alternate(solve(j), scan_pair(j - 1))
    _alternate(scan_pair(npairs_t - 1))

    y = y_s[...]
    inv = 1.0 / RWKV_HEAD
    mean = head_sum(y, 2) * inv
    yc = y - mean
    var = head_sum(yc * yc, 2) * inv
    yn = yc * lax.rsqrt(var + RWKV_GN_EPS) * lng_ref[...] + lnb_ref[...]
    bonus = head_sum(r * k * rk_ref[...], 2) * v
    o_ref[0] = (yn + bonus) * g


def _rwkv_call(proj, mu, w0, w2, a0, a2, g2, k_k, k_a, r_k, ln_g, ln_b, tm=512):
    bsz, t_len, _ = proj.shape
    w = BRANCH_W
    w2p = jnp.zeros((LANES, w), F32).at[:64].set(w2).astype(BF16)
    a2p = jnp.zeros((LANES, w), F32).at[64:].set(a2).astype(BF16)
    vec = lambda n: pl.BlockSpec((1, n), lambda b, t: (0, 0))
    mat = lambda n: pl.BlockSpec((n, w), lambda b, t: (0, 0))
    return pl.pallas_call(
        functools.partial(_rwkv_kernel, tm=tm),
        grid=(bsz, t_len // tm),
        in_specs=[pl.BlockSpec((1, tm, RWKV_COLS), lambda b, t: (b, t, 0)),
                  vec(RWKV_COLS), vec(w), mat(LANES), vec(w), mat(LANES), mat(LANES),
                  vec(w), vec(w), vec(w), vec(w), vec(w)],
        out_specs=pl.BlockSpec((1, tm, w), lambda b, t: (b, t, 0)),
        out_shape=jax.ShapeDtypeStruct((bsz, t_len, w), F32),
        scratch_shapes=[pltpu.VMEM((8, RWKV_COLS), F32), pltpu.VMEM((w // LANES, LANES, LANES), F32)]
        + [pltpu.VMEM((tm, w), BF16) for _ in range(6)] + [pltpu.VMEM((tm, w), F32) for _ in range(3)]
        + [pltpu.VMEM((tm, 2 * w), BF16)],
        compiler_params=_cparams(("parallel", "arbitrary")),
        name="rwkv7",
    )(proj, mu.reshape(1, -1), w0.reshape(1, w), w2p, a0.reshape(1, w), a2p, g2.astype(BF16),
      k_k.reshape(1, w), k_a.reshape(1, w), r_k.reshape(1, w), ln_g.reshape(1, w), ln_b.reshape(1, w))


LOG2E = 1.4426950408889634
FOX_AUG = 3


def _fox_aug_base(h):
    return FOX_DH * (1 - h) + 1


def _foxf_kernel(fl_ref, fb_ref, kaug_ref, carry_ref, *, tm):
    @pl.when(pl.program_id(1) == 0)
    def _():
        carry_ref[...] = jnp.zeros_like(carry_ref)

    lf = _log_sigmoid(fl_ref[0] + fb_ref[...])
    cum = _dot_split_rhs(_tri(tm).astype(BF16), lf, 3) + carry_ref[0:1, :]
    carry_ref[0:1, :] = cum[tm - 1:tm, :]
    lane = lax.broadcasted_iota(jnp.int32, (1, LANES), 1)
    for hh in range(FOX_HEADS):
        base = _fox_aug_base(hh % 2)
        terms = _split_bf16(jnp.broadcast_to(cum[:, hh:hh + 1], cum.shape) * LOG2E, FOX_AUG)
        aug = jnp.where((lane >= base + FOX_AUG) & (lane < base + 2 * FOX_AUG), 1.0, 0.0).astype(BF16)
        aug = jnp.broadcast_to(aug, cum.shape)
        for i, term in enumerate(terms):
            aug = jnp.where(lane == base + i, term, aug)
        kaug_ref[0, hh] = aug


def _foxf_call(fl, f_bias, tm=512):
    bsz, t_len, _ = fl.shape
    fb = jnp.zeros((1, LANES), F32).at[0, :FOX_HEADS].set(f_bias)
    return pl.pallas_call(
        functools.partial(_foxf_kernel, tm=tm),
        grid=(bsz, t_len // tm),
        in_specs=[pl.BlockSpec((1, tm, LANES), lambda b, t: (b, t, 0)),
                  pl.BlockSpec((1, LANES), lambda b, t: (0, 0))],
        out_specs=pl.BlockSpec((1, FOX_HEADS, tm, LANES), lambda b, t: (b, 0, t, 0)),
        out_shape=jax.ShapeDtypeStruct((bsz, FOX_HEADS, t_len, LANES), BF16),
        scratch_shapes=[pltpu.VMEM((8, LANES), F32)],
        compiler_params=_cparams(("parallel", "arbitrary")),
        name="fox_cumf",
    )(fl, fb)


def _fox_kernel(q_ref, k_ref, v_ref, kaug_ref, o_ref, m_s, acc_s, qm_s, *, tq):
    qi = pl.program_id(2)
    lane = lax.broadcasted_iota(jnp.int32, (1, LANES), 1)
    own = [lane < FOX_DH, lane >= FOX_DH]
    m_s[...] = jnp.full_like(m_s, -jnp.inf)
    acc_s[...] = jnp.zeros_like(acc_s)
    q = q_ref[0] * (FOX_DH ** -0.5 * LOG2E)
    for h in range(2):
        base = _fox_aug_base(h)
        first = kaug_ref[0, h, pl.ds(pl.multiple_of(qi * tq, tq), 8), :].astype(F32)
        f0_terms = pltpu.roll(first, FOX_AUG, 1)[0:1, :]
        q_aug = jnp.where((lane >= base) & (lane < base + FOX_AUG), -1.0,
                          jnp.where((lane >= base + FOX_AUG) & (lane < base + 2 * FOX_AUG), f0_terms, 0.0))
        qm_s[h] = jnp.where(own[h], q, q_aug).astype(BF16)
    den = [FOX_DH, 0]

    def step(ki, masked):
        rows = pl.ds(pl.multiple_of(ki * tq, tq), tq)
        kb = k_ref[0, rows, :]
        vb = v_ref[0, rows, :]
        sts = [_dot_nt(jnp.where(own[h], kb, kaug_ref[0, h, rows, :]), qm_s[h]) for h in range(2)]
        yield
        ps, alphas = [], []
        for h in range(2):
            s = sts[h]
            if masked:
                krow = lax.broadcasted_iota(jnp.int32, (tq, tq), 0)
                qcol = lax.broadcasted_iota(jnp.int32, (tq, tq), 1)
                s = jnp.where(krow <= qcol, s, -1e30)
            m_old = m_s[h]
            m_new = jnp.maximum(m_old, jnp.max(s, axis=0, keepdims=True))
            alpha = jnp.exp2(m_old - m_new)
            m_s[h] = m_new
            ps.append(jnp.exp2(s - m_new).astype(BF16))
            alphas.append(alpha)
        yield
        for h in range(2):
            v_den = jnp.where(lane == den[h], jnp.ones_like(vb), vb)
            acc_s[h] = alphas[h] * acc_s[h] + _dot_tn(v_den, ps[h])
        yield

    def earlier_pair(i, carry):
        _alternate(step(2 * i, False), step(2 * i + 1, False))
        return carry

    lax.fori_loop(0, qi // 2, earlier_pair, 0)

    @pl.when(qi % 2 == 1)
    def _():
        _alternate(step(qi - 1, False), step(qi, True))

    @pl.when(qi % 2 == 0)
    def _():
        _alternate(step(qi, True))

    o0 = acc_s[0] / acc_s[0][den[0]:den[0] + 1, :]
    o1 = acc_s[1] / acc_s[1][den[1]:den[1] + 1, :]
    chan = lax.broadcasted_iota(jnp.int32, (LANES, 1), 0)
    o_ref[0] = jnp.where(chan < FOX_DH, o0, o1).T


def _fox_call(q, kv, kaug, tq=512):
    bsz, t_len, _ = q.shape
    npairs = FOX_HEADS // 2
    return pl.pallas_call(
        functools.partial(_fox_kernel, tq=tq),
        grid=(bsz, npairs, t_len // tq),
        in_specs=[
            pl.BlockSpec((1, tq, LANES), lambda b, p, i: (b, i, p)),
            pl.BlockSpec((1, t_len, LANES), lambda b, p, i: (b, 0, p)),
            pl.BlockSpec((1, t_len, LANES), lambda b, p, i: (b, 0, npairs + p)),
            pl.BlockSpec((1, 2, t_len, LANES), lambda b, p, i: (b, p, 0, 0)),
        ],
        out_specs=pl.BlockSpec((1, tq, LANES), lambda b, p, i: (b, i, p)),
        out_shape=jax.ShapeDtypeStruct((bsz, t_len, BRANCH_W), F32),
        scratch_shapes=[pltpu.VMEM((2, 1, tq), F32), pltpu.VMEM((2, LANES, tq), F32),
                        pltpu.VMEM((2, tq, LANES), BF16)],
        compiler_params=_cparams(("parallel", "parallel", "arbitrary")),
        name="fox_attn",
    )(q, kv, kv, kaug)


def _merge_kernel(x_ref, h_ref, ba_ref, bb_ref, bc_ref, bd_ref, mod_ref, wg_ref, bg_ref, wbr_ref, wo_ref,
                  l1g_ref, l1b_ref, rw_ref, rb_ref, x1_ref, h2_ref, cmb_ref):
    tm = x_ref.shape[1]
    half = tm // 2

    def rows_of(r):
        rs = slice(r * half, (r + 1) * half)
        hb = h_ref[0, rs, :]
        merged = None
        for n, br_ref in enumerate((ba_ref, bb_ref, bc_ref, bd_ref)):
            gate = _sigmoid(_dot(hb, wg_ref[n]) + bg_ref[n:n + 1, :])
            term = gate * _dot(br_ref[0, rs, :].astype(BF16), wbr_ref[n])
            merged = term if merged is None else merged + term
        yield
        y = _dot(merged.astype(BF16), wo_ref[...])
        yield
        gt1 = mod_ref[0, 2:3, :]
        x1 = _ln(DN_ALPHA * x_ref[0, rs, :] + gt1 * y) * l1g_ref[...] + l1b_ref[...]
        x1_ref[0, rs, :] = x1
        h2 = _ln(x1) * (1.0 + mod_ref[0, 4:5, :]) + mod_ref[0, 3:4, :]
        h2_ref[0, rs, :] = h2.astype(BF16)

        lane = lax.broadcasted_iota(jnp.int32, (half, LANES), 1)
        lanef = lane.astype(F32)
        neg = -jnp.inf
        h_hi, h_lo = _split_bf16(h2, 2)
        r_hi, r_lo = _split_bf16(rw_ref[...], 2)
        logits = jnp.where(lane < N_EXPERTS, _dot(h_hi, r_hi) + (_dot(h_hi, r_lo) + _dot(h_lo, r_hi)), neg)
        yield
        ex = jnp.exp(logits - jnp.max(logits, axis=-1, keepdims=True))
        probs = ex / jnp.sum(ex, axis=-1, keepdims=True)
        sel = probs + rb_ref[...]
        grp = lane // 4

        def top2(vals):
            m1 = jnp.max(vals, axis=-1, keepdims=True)
            i1 = jnp.min(jnp.where(vals == m1, lanef, 999.0), axis=-1, keepdims=True)
            rest = jnp.where(lanef == i1, neg, vals)
            m2 = jnp.max(rest, axis=-1, keepdims=True)
            i2 = jnp.min(jnp.where(rest == m2, lanef, 999.0), axis=-1, keepdims=True)
            return m1, i1, m2, i2

        best = None
        for gidx in range(4):
            m1, _, m2, _ = top2(jnp.where(grp == gidx, sel, neg))
            score = m1 + m2
            if best is None:
                best, gsel = score, jnp.zeros_like(score, dtype=jnp.int32)
            else:
                better = score > best
                gsel = jnp.where(better, gidx, gsel)
                best = jnp.where(better, score, best)
        _, i1, _, i2 = top2(jnp.where(grp == gsel, sel, neg))
        w1 = jnp.sum(jnp.where(lanef == i1, probs, 0.0), axis=-1, keepdims=True)
        w2 = jnp.sum(jnp.where(lanef == i2, probs, 0.0), axis=-1, keepdims=True)
        cmb_ref[0, rs, :] = ((jnp.where(lanef == i1, w1, 0.0) + jnp.where(lanef == i2, w2, 0.0)) / (w1 + w2)
                             + jnp.where(lane == N_EXPERTS + gsel, 1.0, 0.0))
        yield

    _alternate(rows_of(0), rows_of(1))


def _merge_call(x, h, branches, mod, wg, bg, wbr, wo, l1g, l1b, rw, rb, tm=512):
    bsz, t_len, _ = x.shape
    tok = lambda w: pl.BlockSpec((1, tm, w), lambda b, t: (b, t, 0))
    const = lambda shape: pl.BlockSpec(shape, lambda b, t: (0,) * len(shape), pipeline_mode=pl.Buffered(1))
    return pl.pallas_call(
        _merge_kernel,
        grid=(bsz, t_len // tm),
        in_specs=[tok(D_MODEL), tok(D_MODEL)] + [tok(BRANCH_W)] * 4
        + [pl.BlockSpec((1, N_ADA, D_MODEL), lambda b, t: (b, 0, 0)),
           const((4, D_MODEL, D_MODEL)), const((4, D_MODEL)), const((4, BRANCH_W, D_MODEL)), const((D_MODEL, D_MODEL)),
           const((1, D_MODEL)), const((1, D_MODEL)), const((D_MODEL, LANES)), const((1, LANES))],
        out_specs=[tok(D_MODEL), tok(D_MODEL), tok(LANES)],
        out_shape=[jax.ShapeDtypeStruct((bsz, t_len, D_MODEL), F32),
                   jax.ShapeDtypeStruct((bsz, t_len, D_MODEL), BF16),
                   jax.ShapeDtypeStruct((bsz, t_len, LANES), F32)],
        compiler_params=_cparams(("parallel", "parallel")),
        name="merge",
    )(x, h, *branches, mod, wg, bg, wbr, wo, l1g, l1b, rw, rb)


def _moe_kernel(h_ref, cmb_ref, x_ref, mod_ref, wg_ref, wu_ref, wd_ref, l2g_ref, l2b_ref, o_ref,
                xs_s, cs_s, ys_s, pos_s, seg_s, *, tm):
    step = pl.program_id(2)
    nrow = tm + N_GROUPS * MOE_SEG
    lane = lax.broadcasted_iota(jnp.int32, (1, LANES), 1)

    @pl.when(step == 0)
    def _():
        cmb = cmb_ref[0]
        gh = jnp.where((lane >= N_EXPERTS) & (lane < N_EXPERTS + N_GROUPS), cmb, 0.0)
        tr = lax.broadcasted_iota(jnp.int32, (tm, tm), 0)
        tc = lax.broadcasted_iota(jnp.int32, (tm, tm), 1)
        rank = _dot(jnp.where(tc < tr, 1.0, 0.0).astype(BF16), gh.astype(BF16))
        counts = jnp.sum(gh, axis=0, keepdims=True)
        padded = jnp.floor((counts + (MOE_SEG - 1)) * (1.0 / MOE_SEG)) * MOE_SEG
        lr = lax.broadcasted_iota(jnp.int32, (LANES, LANES), 0)
        lc = lax.broadcasted_iota(jnp.int32, (LANES, LANES), 1)
        before = jnp.where(lr < lc, 1.0, 0.0).astype(BF16)
        start = _dot_split_lhs(jnp.broadcast_to(padded, (8, LANES)), before, 3)[0:1, :]
        pos = jnp.sum(gh * (start + rank), axis=-1, keepdims=True)
        posb = jnp.broadcast_to(pos, (tm, LANES))
        pos_s[...] = posb
        pos_row = posb.T[0:1, :]
        perm = jnp.where(lax.broadcasted_iota(jnp.int32, (nrow, tm), 0).astype(F32) == pos_row, 1.0, 0.0)
        perm = perm.astype(BF16)
        xs_s[...] = _dot(perm, h_ref[0]).astype(BF16)
        cs_s[...] = _dot_split_rhs(perm, cmb, 3)
        ys_s[...] = jnp.zeros_like(ys_s)
        for g in range(N_GROUPS):
            pick = lane == N_EXPERTS + g
            seg_s[g] = jnp.sum(jnp.where(pick, start, 0.0)).astype(jnp.int32)
            seg_s[N_GROUPS + g] = jnp.sum(jnp.where(pick, padded, 0.0)).astype(jnp.int32) // MOE_SEG

    first_e = step * MOE_EPS
    grp = first_e // (N_EXPERTS // N_GROUPS)
    seg_start = seg_s[grp]
    nseg = seg_s[N_GROUPS + grp]

    def block(first_seg, nrows):
        rows = pl.ds(pl.multiple_of(seg_start + first_seg * MOE_SEG, MOE_SEG), nrows)
        xb = xs_s[rows, :]
        cs = cs_s[rows, :]
        y = ys_s[rows, :]
        for j in range(MOE_EPS):
            a = _dot(xb, wg_ref[j])
            u = _dot(xb, wu_ref[j])
            he = (a * _sigmoid(a) * u).astype(BF16)
            ce = jnp.sum(jnp.where(lane == first_e + j, cs, 0.0), axis=-1, keepdims=True)
            y = y + ce * _dot(he, wd_ref[j])
        ys_s[rows, :] = y

    def double_block(i, carry):
        block(2 * i, 2 * MOE_SEG)
        return carry

    lax.fori_loop(0, nseg // 2, double_block, 0)

    @pl.when(nseg % 2 == 1)
    def _():
        block(nseg - 1, MOE_SEG)

    @pl.when(step == N_EXPERTS // MOE_EPS - 1)
    def _():
        back = jnp.where(lax.broadcasted_iota(jnp.int32, (tm, nrow), 1).astype(F32) == pos_s[:, 0:1], 1.0, 0.0)
        y_hi, y_lo = _split_bf16(ys_s[...], 2)
        back = back.astype(BF16)
        y = _dot(back, y_hi) + _dot(back, y_lo)
        gt2 = mod_ref[0, 5:6, :]
        o_ref[0] = _ln(DN_ALPHA * x_ref[0] + gt2 * y) * l2g_ref[...] + l2b_ref[...]


def _moe_call(h2, cmb, x1, mod, wg, wu, wd, l2g, l2b, tm=1024):
    bsz, t_len, _ = x1.shape
    tok = lambda w: pl.BlockSpec((1, tm, w), lambda b, t, e: (b, t, 0))
    nrow = tm + N_GROUPS * MOE_SEG
    return pl.pallas_call(
        functools.partial(_moe_kernel, tm=tm),
        grid=(bsz, t_len // tm, N_EXPERTS // MOE_EPS),
        in_specs=[tok(D_MODEL), tok(LANES), tok(D_MODEL),
                  pl.BlockSpec((1, N_ADA, D_MODEL), lambda b, t, e: (b, 0, 0)),
                  pl.BlockSpec((MOE_EPS, D_MODEL, D_FF_EXPERT), lambda b, t, e: (e, 0, 0)),
                  pl.BlockSpec((MOE_EPS, D_MODEL, D_FF_EXPERT), lambda b, t, e: (e, 0, 0)),
                  pl.BlockSpec((MOE_EPS, D_FF_EXPERT, D_MODEL), lambda b, t, e: (e, 0, 0)),
                  pl.BlockSpec((1, D_MODEL), lambda b, t, e: (0, 0)),
                  pl.BlockSpec((1, D_MODEL), lambda b, t, e: (0, 0))],
        out_specs=tok(D_MODEL),
        out_shape=jax.ShapeDtypeStruct((bsz, t_len, D_MODEL), F32),
        scratch_shapes=[pltpu.VMEM((nrow, D_MODEL), BF16), pltpu.VMEM((nrow, LANES), F32),
                        pltpu.VMEM((nrow, D_MODEL), F32), pltpu.VMEM((tm, LANES), F32),
                        pltpu.SMEM((2 * N_GROUPS,), jnp.int32)],
        compiler_params=_cparams(("parallel", "parallel", "arbitrary")),
        name="moe",
    )(h2, cmb, x1, mod, wg, wu, wd, l2g, l2b)


def _pad_cols(w, n):
    return jnp.pad(w, ((0, 0), (0, n - w.shape[1])))


def kernel(x, c, ada_w, ada_b, w_in, gla_alpha_up, gla_alpha_b, gla_norm_g, rwkv_mu, rwkv_w0, rwkv_w2, rwkv_a0, rwkv_a2, rwkv_g2, rwkv_k_k, rwkv_k_a, rwkv_r_k, rwkv_ln_g, rwkv_ln_b, fox_f_bias, hgrn_lb_logits, hgrn_norm_g, w_br, w_gate, b_gate, w_o, ln1_g, ln1_b, router_w, router_b, exp_w_gate, exp_w_up, exp_w_down, ln2_g, ln2_b):
    mod_all = _ada_call(c, ada_w, ada_b)
    rw = _pad_cols(router_w, LANES)
    rb = _pad_cols(router_b.reshape(1, N_EXPERTS), LANES)
    o1, o2, o3 = GLA_COLS, GLA_COLS + RWKV_COLS, GLA_COLS + RWKV_COLS + FOX_COLS
    for i in range(DEPTH):
        mod = mod_all[i]
        wi = w_in[i]
        wa = _pad_cols(wi[:, :o1], PAD_COLS).astype(BF16)
        wb = wi[:, o1:o2].astype(BF16)
        wc = _pad_cols(wi[:, o2:o3], PAD_COLS).astype(BF16)
        wd = wi[:, o3:].astype(BF16)
        h, pa, pb, pcq, pckv, pcf, pd = _inproj_call(x, mod, wa, wb, wc, wd)
        br_a = _gla_call(pa, gla_alpha_up[i], gla_alpha_b[i], gla_norm_g[i])
        br_b = _rwkv_call(pb, rwkv_mu[i], rwkv_w0[i], rwkv_w2[i], rwkv_a0[i], rwkv_a2[i], rwkv_g2[i],
                          rwkv_k_k[i], rwkv_k_a[i], rwkv_r_k[i], rwkv_ln_g[i], rwkv_ln_b[i])
        br_c = _fox_call(pcq, pckv, _foxf_call(pcf, fox_f_bias[i]))
        br_d = _hgrn_call(pd, hgrn_lb_logits, hgrn_norm_g[i], i)
        x1, h2, cmb = _merge_call(
            x, h, (br_a, br_b, br_c, br_d), mod, w_gate[i].astype(BF16), b_gate[i], w_br[i].astype(BF16),
            w_o[i].astype(BF16), ln1_g[i].reshape(1, -1), ln1_b[i].reshape(1, -1), rw, rb)
        x = _moe_call(h2, cmb, x1, mod, exp_w_gate[i].astype(BF16), exp_w_up[i].astype(BF16),
                      exp_w_down[i].astype(BF16), ln2_g[i].reshape(1, -1), ln2_b[i].reshape(1, -1))
    return x
```

```python
import functools

import jax
import jax.numpy as jnp
from jax import lax
from jax.experimental import pallas as pl
from jax.experimental.pallas import tpu as pltpu

F32 = jnp.float32
BF16 = jnp.bfloat16
HI = lax.Precision.HIGHEST

D_MODEL = 1024
DEPTH = 2
N_ADA = 6
BRANCH_W = 512
GLA_HEADS = 4
GLA_DK = 64
GLA_KW = 256
GLA_GATE_RANK = 16
GLA_LOGIT_NORM = 16.0
RWKV_HEAD = 64
RWKV_COLS = 1792
RWKV_GN_EPS = 64e-5
FOX_HEADS = 8
FOX_DH = 64
HGRN_KW = 512
N_EXPERTS = 16
N_GROUPS = 4
D_FF_EXPERT = 512
MOE_SEG = 128
MOE_EPS = 2
DN_ALPHA = (2.0 * DEPTH) ** 0.25
GLA_COLS = 1552
FOX_COLS = 1544
HGRN_COLS = 2048
PAD_COLS = 1664

LANES = 128
CHUNK = 64
SUB = 16
FACTORED_MAX_DECAY = 80.0
FACTORED_MAX_ABS = 1e3
VMEM_LIMIT = 56 * 1024 * 1024


def _cparams(sem):
    return pltpu.CompilerParams(dimension_semantics=sem, vmem_limit_bytes=VMEM_LIMIT)


def _sigmoid(x):
    return 1.0 / (1.0 + jnp.exp(-x))


def _log_sigmoid(x):
    return jnp.minimum(x, 0.0) - jnp.log(1.0 + jnp.exp(-jnp.abs(x)))


def _ln(x, eps=1e-5):
    mu = jnp.mean(x, axis=-1, keepdims=True)
    xc = x - mu
    var = jnp.mean(xc * xc, axis=-1, keepdims=True)
    return xc * lax.rsqrt(var + eps)


def _tri(n, strict=False):
    r = lax.broadcasted_iota(jnp.int32, (n, n), 0)
    c = lax.broadcasted_iota(jnp.int32, (n, n), 1)
    return jnp.where((c < r) if strict else (c <= r), 1.0, 0.0).astype(F32)


def _dot(a, b, prec=None):
    return jnp.dot(a, b, preferred_element_type=F32, precision=prec)


def _dot_nt(a, b, prec=None):
    return lax.dot_general(a, b, (((1,), (1,)), ((), ())), preferred_element_type=F32, precision=prec)


def _alternate(*gens):
    live = list(gens)
    while live:
        for gen in list(live):
            if next(gen, "done") == "done":
                live.remove(gen)


def _dot_tn(a, b, prec=None):
    return lax.dot_general(a, b, (((0,), (0,)), ((), ())), preferred_element_type=F32, precision=prec)


def _ada_kernel(c_ref, w_ref, b_ref, o_ref):
    c = c_ref[...]
    cond = c * _sigmoid(c)
    o_ref[0] = _dot(cond, w_ref[0], HI) + b_ref[0]


def _ada_call(c, ada_w, ada_b):
    bsz = c.shape[0]
    n = N_ADA * D_MODEL
    bn = 1536
    out = pl.pallas_call(
        _ada_kernel,
        grid=(DEPTH, n // bn),
        in_specs=[
            pl.BlockSpec((bsz, D_MODEL), lambda i, j: (0, 0)),
            pl.BlockSpec((1, D_MODEL, bn), lambda i, j: (i, 0, j)),
            pl.BlockSpec((1, 1, bn), lambda i, j: (i, 0, j)),
        ],
        out_specs=pl.BlockSpec((1, bsz, bn), lambda i, j: (i, 0, j)),
        out_shape=jax.ShapeDtypeStruct((DEPTH, bsz, n), F32),
        compiler_params=_cparams(("parallel", "parallel")),
        name="ada_mod",
    )(c, ada_w, ada_b.reshape(DEPTH, 1, n))
    return out.reshape(DEPTH, bsz, N_ADA, D_MODEL)


def _inproj_kernel(x_ref, mod_ref, wa_ref, wb_ref, wc_ref, wd_ref,
                   h_ref, oa_ref, ob_ref, ocq_ref, ockv_ref, ocf_ref, od_ref):
    x = x_ref[0]
    sh = mod_ref[0, 0:1, :]
    sc = mod_ref[0, 1:2, :]
    hb = (_ln(x) * (1.0 + sc) + sh).astype(BF16)
    h_ref[0] = hb
    oa_ref[0] = _dot(hb, wa_ref[...])
    ob_ref[0] = _dot(hb, wb_ref[...])
    oc = _dot(hb, wc_ref[...])
    w = BRANCH_W
    ocq_ref[0] = oc[:, 0:w]
    ockv_ref[0] = oc[:, w:3 * w].astype(BF16)
    ocf_ref[0] = oc[:, 3 * w:3 * w + LANES]
    od_ref[0] = _dot(hb, wd_ref[...])


def _inproj_call(x, mod, wa, wb, wc, wd, tm=512):
    bsz, t_len, _ = x.shape
    widths = (wa.shape[1], wb.shape[1], wc.shape[1], wd.shape[1])
    outs = [(D_MODEL, BF16), (widths[0], F32), (widths[1], F32), (BRANCH_W, F32), (2 * BRANCH_W, BF16),
            (LANES, F32), (widths[3], F32)]

    def wspec(w):
        return pl.BlockSpec((D_MODEL, w), lambda b, t: (0, 0), pipeline_mode=pl.Buffered(1))

    return pl.pallas_call(
        _inproj_kernel,
        grid=(bsz, t_len // tm),
        in_specs=[pl.BlockSpec((1, tm, D_MODEL), lambda b, t: (b, t, 0)),
                  pl.BlockSpec((1, N_ADA, D_MODEL), lambda b, t: (b, 0, 0))] + [wspec(w) for w in widths],
        out_specs=[pl.BlockSpec((1, tm, w), lambda b, t: (b, t, 0)) for w, _ in outs],
        out_shape=[jax.ShapeDtypeStruct((bsz, t_len, w), dt) for w, dt in outs],
        compiler_params=_cparams(("parallel", "parallel")),
        name="inproj",
    )(x, mod, wa, wb, wc, wd)


def _gla_kernel(*refs, mode, layer, hp, qscale, tm, ngrp):
    if mode == "gla":
        q_ref, k_ref, v_ref, gate_ref, al_ref, aup_ref, ab_ref, ng_ref, sel_ref, o_ref = refs[:10]
    else:
        q_ref, k_ref, v_ref, gate_ref, lbl_ref, ng_ref, sel_ref, o_ref = refs[:8]
    st_ref, q_s, k_s, b_s, qe_s, kd_s, pw_s = refs[-7:]
    c = CHUNK
    nb = c // SUB
    dk = LANES // hp

    @pl.when(pl.program_id(1) == 0)
    def _():
        st_ref[...] = jnp.zeros_like(st_ref)

    q = q_ref[0]
    if qscale != 1.0:
        q = q * qscale
    kin = k_ref[0]
    if mode == "gla":
        g = _log_sigmoid(_dot(al_ref[0], aup_ref[...], HI) + ab_ref[...]) / GLA_LOGIT_NORM
        k = kin
    else:
        lg = lbl_ref[...]
        mx = jnp.max(lg, axis=0, keepdims=True)
        ex = jnp.exp(lg - mx)
        p = ex / jnp.sum(ex, axis=0, keepdims=True)
        cum = p[0:1, :]
        for j in range(1, layer + 1):
            cum = cum + p[j:j + 1, :]
        lb = cum - p[0:1, :]
        g = jnp.log(lb + (1.0 - lb) * _sigmoid(kin))
        k = (1.0 - lb) * _sigmoid(-kin)
    tr = lax.broadcasted_iota(jnp.int32, (tm, tm), 0)
    tc = lax.broadcasted_iota(jnp.int32, (tm, tm), 1)
    same = (tr // c) == (tc // c)
    b = _dot_split_rhs(jnp.where((tc <= tr) & same, 1.0, 0.0).astype(BF16), g, 3)
    nch = tm // c
    b_end = jnp.concatenate([jnp.broadcast_to(b[(ci + 1) * c - 1:(ci + 1) * c, :], (c, b.shape[1]))
                             for ci in range(nch)], axis=0)
    qe = (q * jnp.exp(b)).astype(BF16)
    kd = (k * jnp.exp(b_end - b)).astype(BF16)

    lane = lax.broadcasted_iota(jnp.int32, (1, LANES), 1)
    hmask = [((lane >= h * dk) & (lane < (h + 1) * dk)) for h in range(hp)]
    row = lax.broadcasted_iota(jnp.int32, (c, c), 0)
    col = lax.broadcasted_iota(jnp.int32, (c, c), 1)
    off_mask = (col // SUB) < (row // SUB)
    diag_mask = ((col // SUB) == (row // SUB)) & (col <= row)
    ng = ng_ref[...]

    def head_lanes(x, h):
        return x if hp == 1 else jnp.where(hmask[h], x, jnp.zeros_like(x))

    def finish(rows, hv, o):
        o = o * lax.rsqrt(jnp.mean(o * o, axis=-1, keepdims=True) + 1e-6) * ng
        gt = gate_ref[0, rows, hv * LANES:(hv + 1) * LANES]
        o_ref[0, rows, hv * LANES:(hv + 1) * LANES] = o * (gt * _sigmoid(gt))

    def intra_exact(rows):
        qhs, khats = {}, []
        for gi in range(ngrp):
            gl = slice(gi * LANES, (gi + 1) * LANES)
            qc = q_s[rows, gl]
            kc = k_s[rows, gl]
            bc = b_s[rows, gl]
            rblk = jnp.concatenate([jnp.broadcast_to(bc[m * SUB:m * SUB + 1, :], (SUB, LANES)) for m in range(nb)],
                                   axis=0)
            qh = (qc * jnp.exp(bc - rblk)).astype(BF16)
            for h in range(hp):
                qhs[gi * hp + h] = head_lanes(qh, h)
            khats.append([None] + [(kc * jnp.exp(jnp.minimum(bc[m * SUB:m * SUB + 1, :] - bc, 0.0))).astype(BF16)
                                   for m in range(1, nb)])
            for jj in range(SUB):
                kb = jnp.concatenate([jnp.broadcast_to(kc[m * SUB + jj:m * SUB + jj + 1, :], (SUB, LANES))
                                      for m in range(nb)], axis=0)
                bb = jnp.concatenate([jnp.broadcast_to(bc[m * SUB + jj:m * SUB + jj + 1, :], (SUB, LANES))
                                      for m in range(nb)], axis=0)
                pw_s[gi * c:(gi + 1) * c, jj * LANES:(jj + 1) * LANES] = (
                    qc * kb * jnp.exp(jnp.minimum(bc - bb, 0.0))).astype(BF16)
        a_offs = {}
        for hv in range(ngrp * hp):
            a_offs[hv] = jnp.concatenate(
                [jnp.zeros((SUB, c), F32)]
                + [_dot_nt(qhs[hv][m * SUB:(m + 1) * SUB, :], khats[hv // hp][m]) for m in range(1, nb)], axis=0)
        a_diags = [_dot(pw_s[...], sel_ref[h]) for h in range(hp)]
        out = {}
        for hv in range(ngrp * hp):
            gi, h = hv // hp, hv % hp
            out[hv] = jnp.where(off_mask, a_offs[hv], jnp.where(diag_mask, a_diags[h][gi * c:(gi + 1) * c], 0.0))
        return out

    def chunk(ci, carry):
        r0 = pl.multiple_of(ci * c, c)
        rows = pl.ds(r0, c)
        sts, o_int, upds, vbs = [], {}, {}, {}
        for gi in range(ngrp):
            gl = slice(gi * LANES, (gi + 1) * LANES)
            sts.append(st_ref[gi])
            stb = sts[gi].astype(BF16)
            for h in range(hp):
                hv = gi * hp + h
                vbs[hv] = v_ref[0, rows, hv * LANES:(hv + 1) * LANES].astype(BF16)
                o_int[hv] = _dot_nt(head_lanes(qe_s[rows, gl], h), stb)
                upds[hv] = _dot_tn(vbs[hv], kd_s[rows, gl])
        attn = intra_exact(rows)
        for hv in range(ngrp * hp):
            finish(rows, hv, o_int[hv] + _dot(attn[hv].astype(BF16), vbs[hv]))
        for gi in range(ngrp):
            gl = slice(gi * LANES, (gi + 1) * LANES)
            b_last = b_s[pl.ds(pl.multiple_of(r0 + c - 8, 8), 8), gl][7:8, :]
            new_st = sts[gi] * jnp.exp(b_last)
            for h in range(hp):
                new_st = new_st + head_lanes(upds[gi * hp + h], h)
            st_ref[gi] = new_st
        return carry

    small_decay = jnp.max(jnp.maximum(-b * (1.0 / FACTORED_MAX_DECAY),
                                      jnp.maximum(jnp.abs(q), jnp.abs(k)) * (1.0 / FACTORED_MAX_ABS))) < 1.0

    @pl.when(small_decay)
    def _():
        heads = [(ci, hv) for ci in range(nch) for hv in range(ngrp * hp)]
        b0 = jnp.concatenate([jnp.broadcast_to(b[ci * c:ci * c + 1, :], (c, b.shape[1])) for ci in range(nch)], axis=0)
        bz = b - b0
        qz = (q * jnp.exp(bz)).astype(BF16)
        kz = (k * jnp.exp(-bz)).astype(BF16)
        sl = lambda ci, gi: (slice(ci * c, (ci + 1) * c), slice(gi * LANES, (gi + 1) * LANES))
        vbs, upds, attn, o_intra, o_int = {}, {}, {}, {}, {}
        for ci, hv in heads:
            vbs[ci, hv] = v_ref[0, ci * c:(ci + 1) * c, hv * LANES:(hv + 1) * LANES].astype(BF16)
            upds[ci, hv] = _dot_tn(vbs[ci, hv], kd[sl(ci, hv // hp)])
        for ci, hv in heads:
            attn[ci, hv] = jnp.where(col <= row, _dot_nt(head_lanes(qz[sl(ci, hv // hp)], hv % hp),
                                                           kz[sl(ci, hv // hp)]), 0.0).astype(BF16)
        for ci, hv in heads:
            o_intra[ci, hv] = _dot(attn[ci, hv], vbs[ci, hv])
        sts = [[st_ref[gi]] for gi in range(ngrp)]
        for ci in range(nch):
            for gi in range(ngrp):
                new_st = sts[gi][ci] * jnp.exp(b[(ci + 1) * c - 1:(ci + 1) * c, gi * LANES:(gi + 1) * LANES])
                for h in range(hp):
                    new_st = new_st + head_lanes(upds[ci, gi * hp + h], h)
                sts[gi].append(new_st)
        for gi in range(ngrp):
            st_ref[gi] = sts[gi][nch]
        for ci, hv in heads:
            o_int[ci, hv] = _dot_nt(head_lanes(qe[sl(ci, hv // hp)], hv % hp), sts[hv // hp][ci].astype(BF16))
        for ci, hv in heads:
            finish(slice(ci * c, (ci + 1) * c), hv, o_int[ci, hv] + o_intra[ci, hv])

    @pl.when(jnp.logical_not(small_decay))
    def _():
        q_s[...] = q
        k_s[...] = k
        b_s[...] = b
        qe_s[...] = qe
        kd_s[...] = kd
        lax.fori_loop(0, nch, chunk, 0)


def _pair_select(hp):
    dk = LANES // hp
    r = jnp.arange(SUB * LANES)
    jj = r // LANES
    head = (r % LANES) // dk
    cj = jnp.arange(CHUNK) % SUB
    sel = (jj[None, :, None] == cj[None, None, :]) & (head[None, :, None] == jnp.arange(hp)[:, None, None])
    return sel.astype(BF16)


def _gla_scratch(tm, ngrp):
    w = ngrp * LANES
    return ([pltpu.VMEM((ngrp, LANES, LANES), F32)] + [pltpu.VMEM((tm, w), F32) for _ in range(3)]
            + [pltpu.VMEM((tm, w), BF16) for _ in range(2)] + [pltpu.VMEM((ngrp * CHUNK, SUB * LANES), BF16)])


def _gla_call(proj, alpha_up, alpha_b, norm_g, tm=512):
    bsz, t_len, _ = proj.shape
    hp = 2
    ngrp = GLA_HEADS // hp
    aup = jnp.zeros((LANES, GLA_KW), F32).at[:GLA_GATE_RANK].set(alpha_up)
    blk = lambda w, j: pl.BlockSpec((1, tm, w), lambda b, t: (b, t, j))
    const = lambda shape: pl.BlockSpec(shape, lambda b, t: (0,) * len(shape))
    kern = functools.partial(_gla_kernel, mode="gla", layer=0, hp=hp, qscale=GLA_DK ** -0.5, tm=tm, ngrp=ngrp)
    return pl.pallas_call(
        kern,
        grid=(bsz, t_len // tm),
        in_specs=[blk(GLA_KW, 0), blk(GLA_KW, 1), blk(BRANCH_W, 1), blk(BRANCH_W, 2), blk(LANES, 12),
                  const((LANES, GLA_KW)), const((1, GLA_KW)), const((1, LANES)), const((hp, SUB * LANES, CHUNK))],
        out_specs=blk(BRANCH_W, 0),
        out_shape=jax.ShapeDtypeStruct((bsz, t_len, BRANCH_W), F32),
        scratch_shapes=_gla_scratch(tm, ngrp),
        compiler_params=_cparams(("parallel", "arbitrary")),
        name="gla",
    )(proj, proj, proj, proj, proj, aup, alpha_b.reshape(1, GLA_KW), norm_g.reshape(1, LANES), _pair_select(hp))


def _hgrn_call(proj, lb_logits, norm_g, layer, tm=512):
    bsz, t_len, _ = proj.shape
    ngrp = HGRN_KW // LANES
    blk = lambda j: pl.BlockSpec((1, tm, HGRN_KW), lambda b, t: (b, t, j))
    const = lambda shape: pl.BlockSpec(shape, lambda b, t: (0,) * len(shape))
    kern = functools.partial(_gla_kernel, mode="hgrn", layer=layer, hp=1, qscale=1.0, tm=tm, ngrp=ngrp)
    return pl.pallas_call(
        kern,
        grid=(bsz, t_len // tm),
        in_specs=[blk(0), blk(1), blk(2), blk(3), const((DEPTH, HGRN_KW)), const((1, LANES)),
                  const((1, SUB * LANES, CHUNK))],
        out_specs=blk(0),
        out_shape=jax.ShapeDtypeStruct((bsz, t_len, BRANCH_W), F32),
        scratch_shapes=_gla_scratch(tm, ngrp),
        compiler_params=_cparams(("parallel", "arbitrary")),
        name="hgrn",
    )(proj, proj, proj, proj, lb_logits, norm_g.reshape(1, LANES), _pair_select(1))


def _split_bf16(x, n):
    parts, rest = [], x
    for i in range(n):
        part = rest.astype(BF16)
        parts.append(part)
        if i + 1 < n:
            rest = rest - part.astype(F32)
    return parts


def _dot_split_rhs(a01, x, n):
    out = None
    for part in _split_bf16(x, n):
        term = _dot(a01, part)
        out = term if out is None else out + term
    return out


def _dot_split_lhs(x, b01, n):
    out = None
    for part in _split_bf16(x, n):
        term = _dot(part, b01)
        out = term if out is None else out + term
    return out


def _rwkv_kernel(cols_ref, mu_ref, w0_ref, w2_ref, a0_ref, a2_ref, g2_ref, kk_ref, ka_ref, rk_ref, lng_ref, lnb_ref,
                 o_ref, carry_ref, st_ref, rt_s, at_s, bt_s, kt_s, vb_s, w2_s, eg_s, y_s, w1_s, tops_s, *, tm):
    w = BRANCH_W
    c = CHUNK
    npair = w // LANES

    @pl.when(pl.program_id(1) == 0)
    def _():
        carry_ref[...] = jnp.zeros_like(carry_ref)
        st_ref[...] = jnp.zeros_like(st_ref)

    cols = cols_ref[0]
    rowi = lax.broadcasted_iota(jnp.int32, cols.shape, 0)
    prev = jnp.where(rowi == 0, carry_ref[0:1, :], pltpu.roll(cols, 1, 0))
    carry_ref[0:1, :] = cols[tm - 1:tm, :]
    xs = cols + (prev - cols) * mu_ref[...]
    r = xs[:, 0:w]
    k = xs[:, w:2 * w]
    v = xs[:, 2 * w:3 * w]
    wl = xs[:, 3 * w:3 * w + LANES]
    gl = xs[:, 3 * w + LANES:3 * w + 2 * LANES]
    w_raw = _log_sigmoid(w0_ref[...] + _dot(jnp.tanh(wl).astype(BF16), w2_ref[...])) - 0.5
    lw = -jnp.exp(w_raw)
    a = _sigmoid(a0_ref[...] + _dot(wl.astype(BF16), a2_ref[...]))
    g = _dot(_sigmoid(gl).astype(BF16), g2_ref[...])

    hr = lax.broadcasted_iota(jnp.int32, (LANES, LANES), 0) // RWKV_HEAD
    hc = lax.broadcasted_iota(jnp.int32, (LANES, LANES), 1) // RWKV_HEAD
    bdmask = hr == hc
    bd = jnp.where(bdmask, 1.0, 0.0).astype(BF16)

    def head_sum(t, n):
        return jnp.concatenate([_dot_split_lhs(t[:, p * LANES:(p + 1) * LANES], bd, n) for p in range(npair)], axis=1)

    kk = k * kk_ref[...]
    kk = kk / jnp.maximum(jnp.sqrt(head_sum(kk * kk, 3)), 1e-12)
    k = k * (1.0 + (a - 1.0) * ka_ref[...])

    tr = lax.broadcasted_iota(jnp.int32, (tm, tm), 0)
    tc = lax.broadcasted_iota(jnp.int32, (tm, tm), 1)
    tri = jnp.where((tc <= tr) & ((tr // c) == (tc // c)), 1.0, 0.0).astype(BF16)
    gam = _dot_split_rhs(tri, lw, 3)
    eg = jnp.exp(gam)
    ieg = jnp.exp(-gam)
    rt_s[...] = (r * eg).astype(BF16)
    at_s[...] = (-kk * jnp.exp(gam - lw)).astype(BF16)
    bt_s[...] = (kk * a * ieg).astype(BF16)
    kt_s[...] = (k * ieg).astype(BF16)
    vb_s[...] = v.astype(BF16)
    eg_s[...] = eg

    lane = lax.broadcasted_iota(jnp.int32, (1, LANES), 1)
    lo = lane < RWKV_HEAD
    hmask = [lo, jnp.logical_not(lo)]
    row = lax.broadcasted_iota(jnp.int32, (c, LANES), 0)
    col = lax.broadcasted_iota(jnp.int32, (c, LANES), 1)
    top_mask = (col % c) <= row
    ak_mask = (col >= c) & ((col - c) < row)
    r64 = lax.broadcasted_iota(jnp.int32, (c, c), 0)
    c64 = lax.broadcasted_iota(jnp.int32, (c, c), 1)
    strict = c64 < r64
    eye = jnp.where(c64 == r64, 1.0, 0.0).astype(F32)

    def solve(j):
        chains = []
        for cc in range(2):
            r0 = (2 * j + cc) * c
            for p in range(npair):
                ls = slice(p * LANES, (p + 1) * LANES)
                at = at_s[pl.ds(r0, c), ls]
                vb = vb_s[pl.ds(r0, c), ls]
                ra = jnp.concatenate([rt_s[pl.ds(r0, c), ls], at], axis=0)
                bk = jnp.concatenate([bt_s[pl.ds(r0, c), ls], kt_s[pl.ds(r0, c), ls]], axis=0)
                vv = jnp.concatenate([vb, vb], axis=0)
                for h in range(2):
                    x = _dot_nt(jnp.where(hmask[h], ra, jnp.zeros_like(ra)), bk)
                    tops_s[pl.ds(r0, c), (2 * p + h) * LANES:(2 * p + h + 1) * LANES] = (
                        jnp.where(top_mask, x[0:c], 0.0).astype(BF16))
                    chains.append(dict(r0=r0, p=p, h=h, vv=vv, atm=jnp.where(hmask[h], at, jnp.zeros_like(at)),
                                       aab=jnp.where(strict, x[c:2 * c, 0:c], 0.0),
                                       akm=jnp.where(ak_mask, x[c:2 * c], 0.0).astype(BF16)))
            yield
        for ch in chains:
            ab = ch["aab"].astype(BF16)
            ch["xp"] = _dot(ab, ab)
            ch["t"] = eye + ch["aab"]
        yield
        for ch in chains:
            ch["akv"] = _dot(ch["akm"], ch["vv"])
        yield
        for stage in range(5):
            for ch in chains:
                xb = ch["xp"].astype(BF16)
                if stage < 4:
                    res = _dot(jnp.concatenate([ch["t"].astype(BF16), xb], axis=0), xb)
                    ch["t"] = ch["t"] + res[0:c]
                    ch["xp"] = res[c:2 * c]
                else:
                    ch["t"] = ch["t"] + _dot(ch["t"].astype(BF16), xb)
            yield
        for ch in chains:
            z = jnp.concatenate([ch["akv"].astype(BF16), ch["atm"]], axis=1)
            ch["tz"] = _dot(ch["t"].astype(BF16), z)
        for i in range(0, len(chains), 2):
            c0, c1 = chains[i], chains[i + 1]
            ls = slice(c0["p"] * LANES, (c0["p"] + 1) * LANES)
            w1_s[pl.ds(c0["r0"], c), ls] = jnp.where(lo, c0["tz"][:, 0:LANES], c1["tz"][:, 0:LANES])
            w2_s[pl.ds(c0["r0"], c), ls] = (c0["tz"][:, LANES:2 * LANES] + c1["tz"][:, LANES:2 * LANES]).astype(BF16)
        yield

    def scan(ci_):
        r0 = ci_ * c
        pts, rps, bks, vbs = [], [], [], []
        for p in range(npair):
            ls = slice(p * LANES, (p + 1) * LANES)
            pt = st_ref[p]
            rw = jnp.concatenate([rt_s[pl.ds(r0, c), ls], w2_s[pl.ds(r0, c), ls]], axis=0)
            pts.append(pt)
            rps.append(_dot_nt(rw, pt.astype(BF16)))
        yield
        uvs = []
        for p in range(npair):
            ls = slice(p * LANES, (p + 1) * LANES)
            u = w1_s[pl.ds(r0, c), ls] + rps[p][c:2 * c]
            vb = vb_s[pl.ds(r0, c), ls]
            uvs.append(jnp.concatenate([u.astype(BF16), vb], axis=0))
            bks.append(jnp.concatenate([bt_s[pl.ds(r0, c), ls], kt_s[pl.ds(r0, c), ls]], axis=0))
        for p in range(npair):
            ls = slice(p * LANES, (p + 1) * LANES)
            upd = _dot_tn(uvs[p], bks[p])
            eg_last = eg_s[pl.ds(r0 + c - 1, 1), ls]
            st_ref[p] = jnp.where(bdmask, (pts[p] + upd) * eg_last, 0.0)
        yield
        for p in range(npair):
            ls = slice(p * LANES, (p + 1) * LANES)
            y0 = _dot(tops_s[pl.ds(r0, c), (2 * p) * LANES:(2 * p + 1) * LANES], uvs[p])
            y1 = _dot(tops_s[pl.ds(r0, c), (2 * p + 1) * LANES:(2 * p + 2) * LANES], uvs[p])
            y_s[pl.ds(r0, c), ls] = rps[p][0:c] + jnp.where(lo, y0, y1)
        yield

    def scan_pair(j):
        yield from scan(2 * j)
        yield from scan(2 * j + 1)

    npairs_t = tm // (2 * c)
    _alternate(solve(0))
    for j in range(1, npairs_t):
        _alternate(solve(j), scan_pair(j - 1))
    _alternate(scan_pair(npairs_t - 1))

    y = y_s[...]
    inv = 1.0 / RWKV_HEAD
    mean = head_sum(y, 2) * inv
    yc = y - mean
    var = head_sum(yc * yc, 2) * inv
    yn = yc * lax.rsqrt(var + RWKV_GN_EPS) * lng_ref[...] + lnb_ref[...]
    bonus = head_sum(r * k * rk_ref[...], 2) * v
    o_ref[0] = (yn + bonus) * g


def _rwkv_call(proj, mu, w0, w2, a0, a2, g2, k_k, k_a, r_k, ln_g, ln_b, tm=512):
    bsz, t_len, _ = proj.shape
    w = BRANCH_W
    w2p = jnp.zeros((LANES, w), F32).at[:64].set(w2).astype(BF16)
    a2p = jnp.zeros((LANES, w), F32).at[64:].set(a2).astype(BF16)
    vec = lambda n: pl.BlockSpec((1, n), lambda b, t: (0, 0))
    mat = lambda n: pl.BlockSpec((n, w), lambda b, t: (0, 0))
    return pl.pallas_call(
        functools.partial(_rwkv_kernel, tm=tm),
        grid=(bsz, t_len // tm),
        in_specs=[pl.BlockSpec((1, tm, RWKV_COLS), lambda b, t: (b, t, 0)),
                  vec(RWKV_COLS), vec(w), mat(LANES), vec(w), mat(LANES), mat(LANES),
                  vec(w), vec(w), vec(w), vec(w), vec(w)],
        out_specs=pl.BlockSpec((1, tm, w), lambda b, t: (b, t, 0)),
        out_shape=jax.ShapeDtypeStruct((bsz, t_len, w), F32),
        scratch_shapes=[pltpu.VMEM((8, RWKV_COLS), F32), pltpu.VMEM((w // LANES, LANES, LANES), F32)]
        + [pltpu.VMEM((tm, w), BF16) for _ in range(6)] + [pltpu.VMEM((tm, w), F32) for _ in range(3)]
        + [pltpu.VMEM((tm, 2 * w), BF16)],
        compiler_params=_cparams(("parallel", "arbitrary")),
        name="rwkv7",
    )(proj, mu.reshape(1, -1), w0.reshape(1, w), w2p, a0.reshape(1, w), a2p, g2.astype(BF16),
      k_k.reshape(1, w), k_a.reshape(1, w), r_k.reshape(1, w), ln_g.reshape(1, w), ln_b.reshape(1, w))


LOG2E = 1.4426950408889634
FOX_AUG = 3


def _fox_aug_base(h):
    return FOX_DH * (1 - h) + 1


def _foxf_kernel(fl_ref, fb_ref, kaug_ref, carry_ref, *, tm):
    @pl.when(pl.program_id(1) == 0)
    def _():
        carry_ref[...] = jnp.zeros_like(carry_ref)

    lf = _log_sigmoid(fl_ref[0] + fb_ref[...])
    cum = _dot_split_rhs(_tri(tm).astype(BF16), lf, 3) + carry_ref[0:1, :]
    carry_ref[0:1, :] = cum[tm - 1:tm, :]
    lane = lax.broadcasted_iota(jnp.int32, (1, LANES), 1)
    for hh in range(FOX_HEADS):
        base = _fox_aug_base(hh % 2)
        terms = _split_bf16(jnp.broadcast_to(cum[:, hh:hh + 1], cum.shape) * LOG2E, FOX_AUG)
        aug = jnp.where((lane >= base + FOX_AUG) & (lane < base + 2 * FOX_AUG), 1.0, 0.0).astype(BF16)
        aug = jnp.broadcast_to(aug, cum.shape)
        for i, term in enumerate(terms):
            aug = jnp.where(lane == base + i, term, aug)
        kaug_ref[0, hh] = aug


def _foxf_call(fl, f_bias, tm=512):
    bsz, t_len, _ = fl.shape
    fb = jnp.zeros((1, LANES), F32).at[0, :FOX_HEADS].set(f_bias)
    return pl.pallas_call(
        functools.partial(_foxf_kernel, tm=tm),
        grid=(bsz, t_len // tm),
        in_specs=[pl.BlockSpec((1, tm, LANES), lambda b, t: (b, t, 0)),
                  pl.BlockSpec((1, LANES), lambda b, t: (0, 0))],
        out_specs=pl.BlockSpec((1, FOX_HEADS, tm, LANES), lambda b, t: (b, 0, t, 0)),
        out_shape=jax.ShapeDtypeStruct((bsz, FOX_HEADS, t_len, LANES), BF16),
        scratch_shapes=[pltpu.VMEM((8, LANES), F32)],
        compiler_params=_cparams(("parallel", "arbitrary")),
        name="fox_cumf",
    )(fl, fb)


def _fox_kernel(q_ref, k_ref, v_ref, kaug_ref, o_ref, m_s, acc_s, qm_s, *, tq):
    qi = pl.program_id(2)
    lane = lax.broadcasted_iota(jnp.int32, (1, LANES), 1)
    own = [lane < FOX_DH, lane >= FOX_DH]
    m_s[...] = jnp.full_like(m_s, -jnp.inf)
    acc_s[...] = jnp.zeros_like(acc_s)
    q = q_ref[0] * (FOX_DH ** -0.5 * LOG2E)
    for h in range(2):
        base = _fox_aug_base(h)
        first = kaug_ref[0, h, pl.ds(pl.multiple_of(qi * tq, tq), 8), :].astype(F32)
        f0_terms = pltpu.roll(first, FOX_AUG, 1)[0:1, :]
        q_aug = jnp.where((lane >= base) & (lane < base + FOX_AUG), -1.0,
                          jnp.where((lane >= base + FOX_AUG) & (lane < base + 2 * FOX_AUG), f0_terms, 0.0))
        qm_s[h] = jnp.where(own[h], q, q_aug).astype(BF16)
    den = [FOX_DH, 0]

    def step(ki, masked):
        rows = pl.ds(pl.multiple_of(ki * tq, tq), tq)
        kb = k_ref[0, rows, :]
        vb = v_ref[0, rows, :]
        sts = [_dot_nt(jnp.where(own[h], kb, kaug_ref[0, h, rows, :]), qm_s[h]) for h in range(2)]
        yield
        ps, alphas = [], []
        for h in range(2):
            s = sts[h]
            if masked:
                krow = lax.broadcasted_iota(jnp.int32, (tq, tq), 0)
                qcol = lax.broadcasted_iota(jnp.int32, (tq, tq), 1)
                s = jnp.where(krow <= qcol, s, -1e30)
            m_old = m_s[h]
            m_new = jnp.maximum(m_old, jnp.max(s, axis=0, keepdims=True))
            alpha = jnp.exp2(m_old - m_new)
            m_s[h] = m_new
            ps.append(jnp.exp2(s - m_new).astype(BF16))
            alphas.append(alpha)
        yield
        for h in range(2):
            v_den = jnp.where(lane == den[h], jnp.ones_like(vb), vb)
            acc_s[h] = alphas[h] * acc_s[h] + _dot_tn(v_den, ps[h])
        yield

    def earlier_pair(i, carry):
        _alternate(step(2 * i, False), step(2 * i + 1, False))
        return carry

    lax.fori_loop(0, qi // 2, earlier_pair, 0)

    @pl.when(qi % 2 == 1)
    def _():
        _alternate(step(qi - 1, False), step(qi, True))

    @pl.when(qi % 2 == 0)
    def _():
        _alternate(step(qi, True))

    o0 = acc_s[0] / acc_s[0][den[0]:den[0] + 1, :]
    o1 = acc_s[1] / acc_s[1][den[1]:den[1] + 1, :]
    chan = lax.broadcasted_iota(jnp.int32, (LANES, 1), 0)
    o_ref[0] = jnp.where(chan < FOX_DH, o0, o1).T


def _fox_call(q, kv, kaug, tq=512):
    bsz, t_len, _ = q.shape
    npairs = FOX_HEADS // 2
    return pl.pallas_call(
        functools.partial(_fox_kernel, tq=tq),
        grid=(bsz, npairs, t_len // tq),
        in_specs=[
            pl.BlockSpec((1, tq, LANES), lambda b, p, i: (b, i, p)),
            pl.BlockSpec((1, t_len, LANES), lambda b, p, i: (b, 0, p)),
            pl.BlockSpec((1, t_len, LANES), lambda b, p, i: (b, 0, npairs + p)),
            pl.BlockSpec((1, 2, t_len, LANES), lambda b, p, i: (b, p, 0, 0)),
        ],
        out_specs=pl.BlockSpec((1, tq, LANES), lambda b, p, i: (b, i, p)),
        out_shape=jax.ShapeDtypeStruct((bsz, t_len, BRANCH_W), F32),
        scratch_shapes=[pltpu.VMEM((2, 1, tq), F32), pltpu.VMEM((2, LANES, tq), F32),
                        pltpu.VMEM((2, tq, LANES), BF16)],
        compiler_params=_cparams(("parallel", "parallel", "arbitrary")),
        name="fox_attn",
    )(q, kv, kv, kaug)


def _merge_kernel(x_ref, h_ref, ba_ref, bb_ref, bc_ref, bd_ref, mod_ref, wg_ref, bg_ref, wbr_ref, wo_ref,
                  l1g_ref, l1b_ref, rw_ref, rb_ref, x1_ref, h2_ref, cmb_ref):
    tm = x_ref.shape[1]
    half = tm // 2

    def rows_of(r):
        rs = slice(r * half, (r + 1) * half)
        hb = h_ref[0, rs, :]
        merged = None
        for n, br_ref in enumerate((ba_ref, bb_ref, bc_ref, bd_ref)):
            gate = _sigmoid(_dot(hb, wg_ref[n]) + bg_ref[n:n + 1, :])
            term = gate * _dot(br_ref[0, rs, :].astype(BF16), wbr_ref[n])
            merged = term if merged is None else merged + term
        yield
        y = _dot(merged.astype(BF16), wo_ref[...])
        yield
        gt1 = mod_ref[0, 2:3, :]
        x1 = _ln(DN_ALPHA * x_ref[0, rs, :] + gt1 * y) * l1g_ref[...] + l1b_ref[...]
        x1_ref[0, rs, :] = x1
        h2 = _ln(x1) * (1.0 + mod_ref[0, 4:5, :]) + mod_ref[0, 3:4, :]
        h2_ref[0, rs, :] = h2.astype(BF16)

        lane = lax.broadcasted_iota(jnp.int32, (half, LANES), 1)
        lanef = lane.astype(F32)
        neg = -jnp.inf
        h_hi, h_lo = _split_bf16(h2, 2)
        r_hi, r_lo = _split_bf16(rw_ref[...], 2)
        logits = jnp.where(lane < N_EXPERTS, _dot(h_hi, r_hi) + (_dot(h_hi, r_lo) + _dot(h_lo, r_hi)), neg)
        yield
        ex = jnp.exp(logits - jnp.max(logits, axis=-1, keepdims=True))
        probs = ex / jnp.sum(ex, axis=-1, keepdims=True)
        sel = probs + rb_ref[...]
        grp = lane // 4

        def top2(vals):
            m1 = jnp.max(vals, axis=-1, keepdims=True)
            i1 = jnp.min(jnp.where(vals == m1, lanef, 999.0), axis=-1, keepdims=True)
            rest = jnp.where(lanef == i1, neg, vals)
            m2 = jnp.max(rest, axis=-1, keepdims=True)
            i2 = jnp.min(jnp.where(rest == m2, lanef, 999.0), axis=-1, keepdims=True)
            return m1, i1, m2, i2

        best = None
        for gidx in range(4):
            m1, _, m2, _ = top2(jnp.where(grp == gidx, sel, neg))
            score = m1 + m2
            if best is None:
                best, gsel = score, jnp.zeros_like(score, dtype=jnp.int32)
            else:
                better = score > best
                gsel = jnp.where(better, gidx, gsel)
                best = jnp.where(better, score, best)
        _, i1, _, i2 = top2(jnp.where(grp == gsel, sel, neg))
        w1 = jnp.sum(jnp.where(lanef == i1, probs, 0.0), axis=-1, keepdims=True)
        w2 = jnp.sum(jnp.where(lanef == i2, probs, 0.0), axis=-1, keepdims=True)
        cmb_ref[0, rs, :] = ((jnp.where(lanef == i1, w1, 0.0) + jnp.where(lanef == i2, w2, 0.0)) / (w1 + w2)
                             + jnp.where(lane == N_EXPERTS + gsel, 1.0, 0.0))
        yield

    _alternate(rows_of(0), rows_of(1))


def _merge_call(x, h, branches, mod, wg, bg, wbr, wo, l1g, l1b, rw, rb, tm=512):
    bsz, t_len, _ = x.shape
    tok = lambda w: pl.BlockSpec((1, tm, w), lambda b, t: (b, t, 0))
    const = lambda shape: pl.BlockSpec(shape, lambda b, t: (0,) * len(shape), pipeline_mode=pl.Buffered(1))
    return pl.pallas_call(
        _merge_kernel,
        grid=(bsz, t_len // tm),
        in_specs=[tok(D_MODEL), tok(D_MODEL)] + [tok(BRANCH_W)] * 4
        + [pl.BlockSpec((1, N_ADA, D_MODEL), lambda b, t: (b, 0, 0)),
           const((4, D_MODEL, D_MODEL)), const((4, D_MODEL)), const((4, BRANCH_W, D_MODEL)), const((D_MODEL, D_MODEL)),
           const((1, D_MODEL)), const((1, D_MODEL)), const((D_MODEL, LANES)), const((1, LANES))],
        out_specs=[tok(D_MODEL), tok(D_MODEL), tok(LANES)],
        out_shape=[jax.ShapeDtypeStruct((bsz, t_len, D_MODEL), F32),
                   jax.ShapeDtypeStruct((bsz, t_len, D_MODEL), BF16),
                   jax.ShapeDtypeStruct((bsz, t_len, LANES), F32)],
        compiler_params=_cparams(("parallel", "parallel")),
        name="merge",
    )(x, h, *branches, mod, wg, bg, wbr, wo, l1g, l1b, rw, rb)


def _moe_kernel(h_ref, cmb_ref, x_ref, mod_ref, wg_ref, wu_ref, wd_ref, l2g_ref, l2b_ref, o_ref,
                xs_s, cs_s, ys_s, pos_s, seg_s, *, tm):
    step = pl.program_id(2)
    nrow = tm + N_GROUPS * MOE_SEG
    lane = lax.broadcasted_iota(jnp.int32, (1, LANES), 1)

    @pl.when(step == 0)
    def _():
        cmb = cmb_ref[0]
        gh = jnp.where((lane >= N_EXPERTS) & (lane < N_EXPERTS + N_GROUPS), cmb, 0.0)
        tr = lax.broadcasted_iota(jnp.int32, (tm, tm), 0)
        tc = lax.broadcasted_iota(jnp.int32, (tm, tm), 1)
        rank = _dot(jnp.where(tc < tr, 1.0, 0.0).astype(BF16), gh.astype(BF16))
        counts = jnp.sum(gh, axis=0, keepdims=True)
        padded = jnp.floor((counts + (MOE_SEG - 1)) * (1.0 / MOE_SEG)) * MOE_SEG
        lr = lax.broadcasted_iota(jnp.int32, (LANES, LANES), 0)
        lc = lax.broadcasted_iota(jnp.int32, (LANES, LANES), 1)
        before = jnp.where(lr < lc, 1.0, 0.0).astype(BF16)
        start = _dot_split_lhs(jnp.broadcast_to(padded, (8, LANES)), before, 3)[0:1, :]
        pos = jnp.sum(gh * (start + rank), axis=-1, keepdims=True)
        posb = jnp.broadcast_to(pos, (tm, LANES))
        pos_s[...] = posb
        pos_row = posb.T[0:1, :]
        perm = jnp.where(lax.broadcasted_iota(jnp.int32, (nrow, tm), 0).astype(F32) == pos_row, 1.0, 0.0)
        perm = perm.astype(BF16)
        xs_s[...] = _dot(perm, h_ref[0]).astype(BF16)
        cs_s[...] = _dot_split_rhs(perm, cmb, 3)
        ys_s[...] = jnp.zeros_like(ys_s)
        for g in range(N_GROUPS):
            pick = lane == N_EXPERTS + g
            seg_s[g] = jnp.sum(jnp.where(pick, start, 0.0)).astype(jnp.int32)
            seg_s[N_GROUPS + g] = jnp.sum(jnp.where(pick, padded, 0.0)).astype(jnp.int32) // MOE_SEG

    first_e = step * MOE_EPS
    grp = first_e // (N_EXPERTS // N_GROUPS)
    seg_start = seg_s[grp]
    nseg = seg_s[N_GROUPS + grp]

    def block(first_seg, nrows):
        rows = pl.ds(pl.multiple_of(seg_start + first_seg * MOE_SEG, MOE_SEG), nrows)
        xb = xs_s[rows, :]
        cs = cs_s[rows, :]
        y = ys_s[rows, :]
        for j in range(MOE_EPS):
            a = _dot(xb, wg_ref[j])
            u = _dot(xb, wu_ref[j])
            he = (a * _sigmoid(a) * u).astype(BF16)
            ce = jnp.sum(jnp.where(lane == first_e + j, cs, 0.0), axis=-1, keepdims=True)
            y = y + ce * _dot(he, wd_ref[j])
        ys_s[rows, :] = y

    def double_block(i, carry):
        block(2 * i, 2 * MOE_SEG)
        return carry

    lax.fori_loop(0, nseg // 2, double_block, 0)

    @pl.when(nseg % 2 == 1)
    def _():
        block(nseg - 1, MOE_SEG)

    @pl.when(step == N_EXPERTS // MOE_EPS - 1)
    def _():
        back = jnp.where(lax.broadcasted_iota(jnp.int32, (tm, nrow), 1).astype(F32) == pos_s[:, 0:1], 1.0, 0.0)
        y_hi, y_lo = _split_bf16(ys_s[...], 2)
        back = back.astype(BF16)
        y = _dot(back, y_hi) + _dot(back, y_lo)
        gt2 = mod_ref[0, 5:6, :]
        o_ref[0] = _ln(DN_ALPHA * x_ref[0] + gt2 * y) * l2g_ref[...] + l2b_ref[...]


def _moe_call(h2, cmb, x1, mod, wg, wu, wd, l2g, l2b, tm=1024):
    bsz, t_len, _ = x1.shape
    tok = lambda w: pl.BlockSpec((1, tm, w), lambda b, t, e: (b, t, 0))
    nrow = tm + N_GROUPS * MOE_SEG
    return pl.pallas_call(
        functools.partial(_moe_kernel, tm=tm),
        grid=(bsz, t_len // tm, N_EXPERTS // MOE_EPS),
        in_specs=[tok(D_MODEL), tok(LANES), tok(D_MODEL),
                  pl.BlockSpec((1, N_ADA, D_MODEL), lambda b, t, e: (b, 0, 0)),
                  pl.BlockSpec((MOE_EPS, D_MODEL, D_FF_EXPERT), lambda b, t, e: (e, 0, 0)),
                  pl.BlockSpec((MOE_EPS, D_MODEL, D_FF_EXPERT), lambda b, t, e: (e, 0, 0)),
                  pl.BlockSpec((MOE_EPS, D_FF_EXPERT, D_MODEL), lambda b, t, e: (e, 0, 0)),
                  pl.BlockSpec((1, D_MODEL), lambda b, t, e: (0, 0)),
                  pl.BlockSpec((1, D_MODEL), lambda b, t, e: (0, 0))],
        out_specs=tok(D_MODEL),
        out_shape=jax.ShapeDtypeStruct((bsz, t_len, D_MODEL), F32),
        scratch_shapes=[pltpu.VMEM((nrow, D_MODEL), BF16), pltpu.VMEM((nrow, LANES), F32),
                        pltpu.VMEM((nrow, D_MODEL), F32), pltpu.VMEM((tm, LANES), F32),
                        pltpu.SMEM((2 * N_GROUPS,), jnp.int32)],
        compiler_params=_cparams(("parallel", "parallel", "arbitrary")),
        name="moe",
    )(h2, cmb, x1, mod, wg, wu, wd, l2g, l2b)


def _pad_cols(w, n):
    return jnp.pad(w, ((0, 0), (0, n - w.shape[1])))


def kernel(x, c, ada_w, ada_b, w_in, gla_alpha_up, gla_alpha_b, gla_norm_g, rwkv_mu, rwkv_w0, rwkv_w2, rwkv_a0, rwkv_a2, rwkv_g2, rwkv_k_k, rwkv_k_a, rwkv_r_k, rwkv_ln_g, rwkv_ln_b, fox_f_bias, hgrn_lb_logits, hgrn_norm_g, w_br, w_gate, b_gate, w_o, ln1_g, ln1_b, router_w, router_b, exp_w_gate, exp_w_up, exp_w_down, ln2_g, ln2_b):
    mod_all = _ada_call(c, ada_w, ada_b)
    rw = _pad_cols(router_w, LANES)
    rb = _pad_cols(router_b.reshape(1, N_EXPERTS), LANES)
    o1, o2, o3 = GLA_COLS, GLA_COLS + RWKV_COLS, GLA_COLS + RWKV_COLS + FOX_COLS
    for i in range(DEPTH):
        mod = mod_all[i]
        wi = w_in[i]
        wa = _pad_cols(wi[:, :o1], PAD_COLS).astype(BF16)
        wb = wi[:, o1:o2].astype(BF16)
        wc = _pad_cols(wi[:, o2:o3], PAD_COLS).astype(BF16)
        wd = wi[:, o3:].astype(BF16)
        h, pa, pb, pcq, pckv, pcf, pd = _inproj_call(x, mod, wa, wb, wc, wd)
        br_a = _gla_call(pa, gla_alpha_up[i], gla_alpha_b[i], gla_norm_g[i])
        br_b = _rwkv_call(pb, rwkv_mu[i], rwkv_w0[i], rwkv_w2[i], rwkv_a0[i], rwkv_a2[i], rwkv_g2[i],
                          rwkv_k_k[i], rwkv_k_a[i], rwkv_r_k[i], rwkv_ln_g[i], rwkv_ln_b[i])
        br_c = _fox_call(pcq, pckv, _foxf_call(pcf, fox_f_bias[i]))
        br_d = _hgrn_call(pd, hgrn_lb_logits, hgrn_norm_g[i], i)
        x1, h2, cmb = _merge_call(
            x, h, (br_a, br_b, br_c, br_d), mod, w_gate[i].astype(BF16), b_gate[i], w_br[i].astype(BF16),
            w_o[i].astype(BF16), ln1_g[i].reshape(1, -1), ln1_b[i].reshape(1, -1), rw, rb)
        x = _moe_call(h2, cmb, x1, mod, exp_w_gate[i].astype(BF16), exp_w_up[i].astype(BF16),
                      exp_w_down[i].astype(BF16), ln2_g[i].reshape(1, -1), ln2_b[i].reshape(1, -1))
    return x
```
